```python
import math
import jax, jax.numpy as jnp
from jax import lax
import numpy as np

D_MODEL = 2048
BATCH = 2
SEQ = 8192
DEPTH = 2

CHUNK = 64
N_MIXERS = 2
N_RET_LAYERS = (DEPTH + 1) // 2
N_SSD_LAYERS = DEPTH // 2
RMS_EPS = 1e-6

RET_HEAD_DK = 256
RET_HEADS = D_MODEL // RET_HEAD_DK
RET_QK = RET_HEADS * RET_HEAD_DK
RET_HEAD_DV = 2 * RET_HEAD_DK
RET_V = RET_HEADS * RET_HEAD_DV
RET_IN = 2 * RET_QK + 2 * RET_V
ROPE_BASE = 10000.0
GN_EPS = 1e-5

SSD_EXPAND = 2
SSD_D_INNER = SSD_EXPAND * D_MODEL
SSD_HEADDIM = 64
SSD_HEADS = SSD_D_INNER // SSD_HEADDIM
SSD_GROUPS = 8
SSD_HEADS_PER_GROUP = SSD_HEADS // SSD_GROUPS
SSD_STATE = 128
SSD_CONV_W = 4
SSD_CONV_DIM = SSD_D_INNER + 2 * SSD_GROUPS * SSD_STATE
SSD_IN = SSD_D_INNER + SSD_CONV_DIM + SSD_HEADS
SSD_NORM_GROUPS = SSD_GROUPS

D_FF = 4 * D_MODEL

kernel_name = "hybrid_retention_ssd_sandwich_trunk"


def rms_norm(x, w):
    xf = x.astype(jnp.float32)
    y = xf * lax.rsqrt(jnp.mean(xf * xf, axis=-1, keepdims=True) + RMS_EPS)
    return (y * w.astype(jnp.float32)).astype(x.dtype)


def to_chunks(t):
    b, s = t.shape[:2]
    return jnp.moveaxis(t.reshape(b, s // CHUNK, CHUNK, *t.shape[2:]), 1, 0)


def from_chunks(t):
    t = jnp.moveaxis(t, 0, 1)
    return t.reshape(t.shape[0], t.shape[1] * t.shape[2], *t.shape[3:])


def rotary(t, pos):
    half = t.shape[-1] // 2
    inv_freq = ROPE_BASE ** (-jnp.arange(half, dtype=jnp.float32) / half)
    ang = pos.astype(jnp.float32)[:, None] * inv_freq[None, :]
    cos = jnp.cos(ang)[None, :, None, :]
    sin = jnp.sin(ang)[None, :, None, :]
    t1, t2 = t[..., :half], t[..., half:]
    return jnp.concatenate([t1 * cos - t2 * sin, t1 * sin + t2 * cos], axis=-1)


def retention_mixer(u, w_in, gn_w, w_out):
    b, s, _ = u.shape
    f32 = jnp.float32
    proj = u @ w_in
    q, k, v, g = jnp.split(proj, [RET_QK, 2 * RET_QK, 2 * RET_QK + RET_V], axis=-1)
    pos = jnp.arange(s)
    q = rotary(q.astype(f32).reshape(b, s, RET_HEADS, RET_HEAD_DK), pos)
    k = rotary(k.astype(f32).reshape(b, s, RET_HEADS, RET_HEAD_DK), pos) * (RET_HEAD_DK ** -0.5)
    v = v.astype(f32).reshape(b, s, RET_HEADS, RET_HEAD_DV)

    log_gamma = jnp.log1p(-jnp.exp2(-5.0 - jnp.arange(RET_HEADS, dtype=f32)))
    idx = jnp.arange(CHUNK, dtype=f32)
    dist = jnp.abs(idx[:, None] - idx[None, :])
    dmat = jnp.exp(dist[None] * log_gamma[:, None, None])
    xi = jnp.exp((idx[:, None] + 1.0) * log_gamma[None, :])
    zeta = jnp.exp((CHUNK - 1.0 - idx)[:, None] * log_gamma[None, :])
    chunk_decay = jnp.exp(CHUNK * log_gamma)

    def step(state, inp):
        qc, kc, vc = inp
        scores = jnp.einsum('blhd,bmhd->bhlm', qc, kc) * dmat[None]
        inner = jnp.einsum('bhlm,bmhv->blhv', scores, vc)
        cross = jnp.einsum('blhd,bhdv->blhv', qc, state) * xi[None, :, :, None]
        state = state * chunk_decay[None, :, None, None] + jnp.einsum(
            'bmhd,bmhv->bhdv', kc * zeta[None, :, :, None], vc)
        return state, inner + cross

    state0 = jnp.zeros((b, RET_HEADS, RET_HEAD_DK, RET_HEAD_DV), f32)
    _, o = lax.scan(step, state0, (to_chunks(q), to_chunks(k), to_chunks(v)))
    o = from_chunks(o)
    mu = jnp.mean(o, axis=-1, keepdims=True)
    var = jnp.mean(jnp.square(o - mu), axis=-1, keepdims=True)
    o = ((o - mu) * lax.rsqrt(var + GN_EPS)).reshape(b, s, RET_V) * gn_w.astype(f32)
    y = jax.nn.silu(g.astype(f32)) * o
    return y.astype(u.dtype) @ w_out


def causal_depthwise_conv(t, w, bias):
    s = t.shape[1]
    width = w.shape[0]
    tp = jnp.pad(t, ((0, 0), (width - 1, 0), (0, 0)))
    out = bias[None, None, :]
    for tap in range(width):
        out = out + tp[:, tap:tap + s, :] * w[tap][None, None, :]
    return out


def ssd_mixer(u, w_in, conv_w, conv_b, dt_bias, a_log, d_skip, norm_w, w_out):
    b, s, _ = u.shape
    f32 = jnp.float32
    G, R, P, N = SSD_GROUPS, SSD_HEADS_PER_GROUP, SSD_HEADDIM, SSD_STATE
    proj = u @ w_in
    z, xbc, dt = jnp.split(proj, [SSD_D_INNER, SSD_D_INNER + SSD_CONV_DIM], axis=-1)
    xbc = jax.nn.silu(causal_depthwise_conv(xbc, conv_w, conv_b))
    xs, bm, cm = jnp.split(xbc, [SSD_D_INNER, SSD_D_INNER + G * N], axis=-1)
    xs = xs.astype(f32).reshape(b, s, G, R, P)
    bm = bm.astype(f32).reshape(b, s, G, N)
    cm = cm.astype(f32).reshape(b, s, G, N)
    dt = jax.nn.softplus(dt.astype(f32) + dt_bias.astype(f32)).reshape(b, s, G, R)
    a = -jnp.exp(a_log.astype(f32)).reshape(G, R)
    da = dt * a[None, None]
    xdt = xs * dt[..., None]
    tril = jnp.tril(jnp.ones((CHUNK, CHUNK), dtype=bool))

    def step(state, inp):
        xc, ac, bc, cc = inp
        acum = jnp.cumsum(ac, axis=1)
        seg = acum[:, :, None] - acum[:, None, :]
        lmat = jnp.exp(jnp.where(tril[None, :, :, None, None], seg, -jnp.inf))
        cb = jnp.einsum('blgn,bsgn->blsg', cc, bc)
        y_diag = jnp.einsum('blsgr,bsgrp->blgrp', cb[..., None] * lmat, xc)
        y_off = jnp.einsum('blgn,bgrpn->blgrp', cc, state) * jnp.exp(acum)[..., None]
        decay_to_end = jnp.exp(acum[:, -1:] - acum)
        state = state * jnp.exp(acum[:, -1])[..., None, None] + jnp.einsum(
            'bsgn,bsgrp->bgrpn', bc, xc * decay_to_end[..., None])
        return state, y_diag + y_off

    state0 = jnp.zeros((b, G, R, P, N), f32)
    _, ys = lax.scan(step, state0, (to_chunks(xdt), to_chunks(da), to_chunks(bm), to_chunks(cm)))
    y = from_chunks(ys) + d_skip.astype(f32).reshape(G, R)[None, None, :, :, None] * xs
    y = y.reshape(b, s, SSD_D_INNER) * jax.nn.silu(z.astype(f32))
    yg = y.reshape(b, s, SSD_NORM_GROUPS, SSD_D_INNER // SSD_NORM_GROUPS)
    yg = yg * lax.rsqrt(jnp.mean(yg * yg, axis=-1, keepdims=True) + RMS_EPS)
    y = yg.reshape(b, s, SSD_D_INNER) * norm_w.astype(f32)
    return y.astype(u.dtype) @ w_out


def sq_relu_mlp(u, w_up, w_down):
    h = jax.nn.relu(u @ w_up)
    return (h * h) @ w_down


def setup_inputs(seed: int = 0) -> dict:
    key = jax.random.key(seed)
    ks = jax.random.split(key, 20)
    f32 = jnp.float32

    def nrm(k, shape, scale):
        return jax.random.normal(k, shape, f32) * scale

    def gain(k, shape):
        return 1.0 + 0.05 * jax.random.normal(k, shape, f32)

    x = jax.random.normal(ks[0], (BATCH, SEQ, D_MODEL), f32)
    norm_mix_pre = gain(ks[1], (DEPTH, D_MODEL))
    norm_mix_post = gain(ks[2], (DEPTH, D_MODEL))
    norm_ffn_pre = gain(ks[3], (DEPTH, D_MODEL))
    norm_ffn_post = gain(ks[4], (DEPTH, D_MODEL))

    ret_w_in = nrm(ks[5], (N_RET_LAYERS, D_MODEL, RET_IN), D_MODEL ** -0.5)
    ret_gn_w = gain(ks[6], (N_RET_LAYERS, RET_V))
    ret_w_out = nrm(ks[7], (N_RET_LAYERS, RET_V, D_MODEL), RET_V ** -0.5)

    ssd_w_in = nrm(ks[8], (N_SSD_LAYERS, D_MODEL, SSD_IN), D_MODEL ** -0.5)
    ssd_conv_w = nrm(ks[9], (N_SSD_LAYERS, SSD_CONV_W, SSD_CONV_DIM), SSD_CONV_W ** -0.5)
    ssd_conv_b = nrm(ks[10], (N_SSD_LAYERS, SSD_CONV_DIM), 0.02)
    dt0 = jnp.exp(jax.random.uniform(ks[11], (N_SSD_LAYERS, SSD_HEADS), f32,
                                     math.log(1e-3), math.log(1e-1)))
    ssd_dt_bias = dt0 + jnp.log(-jnp.expm1(-dt0))
    ssd_a_log = jnp.log(jax.random.uniform(ks[12], (N_SSD_LAYERS, SSD_HEADS), f32, 1.0, 16.0))
    ssd_d = gain(ks[13], (N_SSD_LAYERS, SSD_HEADS))
    ssd_norm_w = gain(ks[14], (N_SSD_LAYERS, SSD_D_INNER))
    ssd_w_out = nrm(ks[15], (N_SSD_LAYERS, SSD_D_INNER, D_MODEL), SSD_D_INNER ** -0.5)

    mlp_w_up = nrm(ks[16], (DEPTH, D_MODEL, D_FF), D_MODEL ** -0.5)
    mlp_w_down = nrm(ks[17], (DEPTH, D_FF, D_MODEL), D_FF ** -0.5)

    return {"x": x,
            "norm_mix_pre": norm_mix_pre, "norm_mix_post": norm_mix_post,
            "norm_ffn_pre": norm_ffn_pre, "norm_ffn_post": norm_ffn_post,
            "ret_w_in": ret_w_in, "ret_gn_w": ret_gn_w, "ret_w_out": ret_w_out,
            "ssd_w_in": ssd_w_in, "ssd_conv_w": ssd_conv_w, "ssd_conv_b": ssd_conv_b,
            "ssd_dt_bias": ssd_dt_bias, "ssd_a_log": ssd_a_log, "ssd_d": ssd_d,
            "ssd_norm_w": ssd_norm_w, "ssd_w_out": ssd_w_out,
            "mlp_w_up": mlp_w_up, "mlp_w_down": mlp_w_down}


def reference(x, norm_mix_pre, norm_mix_post, norm_ffn_pre, norm_ffn_post,
              ret_w_in, ret_gn_w, ret_w_out,
              ssd_w_in, ssd_conv_w, ssd_conv_b, ssd_dt_bias, ssd_a_log, ssd_d,
              ssd_norm_w, ssd_w_out, mlp_w_up, mlp_w_down):
    h = x
    for i in range(DEPTH):
        j = i // N_MIXERS
        u = rms_norm(h, norm_mix_pre[i])
        if i % N_MIXERS == 0:
            m = retention_mixer(u, ret_w_in[j], ret_gn_w[j], ret_w_out[j])
        else:
            m = ssd_mixer(u, ssd_w_in[j], ssd_conv_w[j], ssd_conv_b[j], ssd_dt_bias[j],
                          ssd_a_log[j], ssd_d[j], ssd_norm_w[j], ssd_w_out[j])
        h = h + rms_norm(m, norm_mix_post[i])
        u = rms_norm(h, norm_ffn_pre[i])
        h = h + rms_norm(sq_relu_mlp(u, mlp_w_up[i], mlp_w_down[i]), norm_ffn_post[i])
    return h
```

```python
import functools
import math

import jax
import jax.numpy as jnp
from jax import lax
from jax.experimental import pallas as pl
from jax.experimental.pallas import tpu as pltpu

F32 = jnp.float32
BF16 = jnp.bfloat16

RMS_EPS = 1e-6
GN_EPS = 1e-5
ROPE_BASE = 10000.0
CHUNK = 64

RET_HEAD_DK = 256
RET_HEAD_DV = 512
SSD_HEADDIM = 64
SSD_GROUPS = 8
SSD_HEADS_PER_GROUP = 8
SSD_STATE = 128
SSD_CONV_W = 4
SSD_GROUP_WIDTH = SSD_HEADS_PER_GROUP * SSD_HEADDIM

V7X_LANES = 128
V7X_SUBLANES = 8
V7X_VMEM_LIMIT_BYTES = 56 * 1024 * 1024

PROJ_TM, PROJ_TN = 1024, 512
OUT_TM = 256
MLP_TM, MLP_TF = 512, 512
RET_L = 256
SSD_L = 256


def _params(semantics):
    return pltpu.CompilerParams(dimension_semantics=semantics,
                                vmem_limit_bytes=V7X_VMEM_LIMIT_BYTES)


def _rms_scale(x):
    return lax.rsqrt(jnp.mean(x * x, axis=-1, keepdims=True) + RMS_EPS)


def _norm_matmul_kernel(x_ref, g_ref, w_ref, o_ref, u_ref):
    @pl.when(pl.program_id(1) == 0)
    def _():
        x = x_ref[...]
        u_ref[...] = (x * _rms_scale(x) * g_ref[...]).astype(BF16)

    o_ref[...] = jnp.dot(u_ref[...], w_ref[...],
                         preferred_element_type=F32).astype(o_ref.dtype)


def _norm_matmul2_kernel(x_ref, g_ref, w_ref, w2_ref, o_ref, o2_ref, u_ref):
    @pl.when(pl.program_id(1) == 0)
    def _():
        x = x_ref[...]
        u = (x * _rms_scale(x) * g_ref[...]).astype(BF16)
        u_ref[...] = u
        o2_ref[...] = jnp.dot(u, w2_ref[...], preferred_element_type=F32)

    o_ref[...] = jnp.dot(u_ref[...], w_ref[...],
                         preferred_element_type=F32).astype(o_ref.dtype)


def norm_matmul(h, g, w, w2=None, *, out_dtype, name):
    t, d = h.shape
    n = w.shape[1]
    tm, tn = PROJ_TM, PROJ_TN
    in_specs = [pl.BlockSpec((tm, d), lambda i, j: (i, 0)),
                pl.BlockSpec((1, d), lambda i, j: (0, 0)),
                pl.BlockSpec((d, tn), lambda i, j: (0, j))]
    out_specs = pl.BlockSpec((tm, tn), lambda i, j: (i, j))
    out_shape = jax.ShapeDtypeStruct((t, n), out_dtype)
    args = [h, g.reshape(1, d), w]
    body = _norm_matmul_kernel
    if w2 is not None:
        n2 = w2.shape[1]
        in_specs.append(pl.BlockSpec((d, n2), lambda i, j: (0, 0)))
        out_specs = [out_specs, pl.BlockSpec((tm, n2), lambda i, j: (i, 0))]
        out_shape = [out_shape, jax.ShapeDtypeStruct((t, n2), F32)]
        args.append(w2)
        body = _norm_matmul2_kernel
    return pl.pallas_call(
        body,
        grid=(t // tm, n // tn),
        in_specs=in_specs,
        out_specs=out_specs,
        out_shape=out_shape,
        scratch_shapes=[pltpu.VMEM((tm, d), BF16)],
        compiler_params=_params(("parallel", "arbitrary")),
        name=name,
    )(*args)


def _out_proj_kernel(y_ref, w_ref, h_ref, g_ref, o_ref):
    m = jnp.dot(y_ref[...], w_ref[...], preferred_element_type=F32)
    o_ref[...] = h_ref[...] + m * _rms_scale(m) * g_ref[...]


def out_proj_residual(y, w, h, g, *, name):
    t, k = y.shape
    d = w.shape[1]
    tm = OUT_TM
    return pl.pallas_call(
        _out_proj_kernel,
        grid=(t // tm,),
        in_specs=[pl.BlockSpec((tm, k), lambda i: (i, 0)),
                  pl.BlockSpec((k, d), lambda i: (0, 0), pipeline_mode=pl.Buffered(1)),
                  pl.BlockSpec((tm, d), lambda i: (i, 0)),
                  pl.BlockSpec((1, d), lambda i: (0, 0))],
        out_specs=pl.BlockSpec((tm, d), lambda i: (i, 0)),
        out_shape=jax.ShapeDtypeStruct((t, d), F32),
        compiler_params=_params(("parallel",)),
        name=name,
    )(y, w, h, g.reshape(1, d))


def _mlp_kernel(x_ref, gpre_ref, wup_ref, wdn_ref, gpost_ref, o_ref, u_ref, acc_ref):
    j = pl.program_id(1)

    @pl.when(j == 0)
    def _():
        x = x_ref[...]
        u_ref[...] = (x * _rms_scale(x) * gpre_ref[...]).astype(BF16)
        acc_ref[...] = jnp.zeros_like(acc_ref)

    hid = jnp.maximum(jnp.dot(u_ref[...], wup_ref[...], preferred_element_type=F32), 0.0)
    acc_ref[...] += jnp.dot((hid * hid).astype(BF16), wdn_ref[...],
                            preferred_element_type=F32)

    @pl.when(j == pl.num_programs(1) - 1)
    def _():
        m = acc_ref[...]
        o_ref[...] = x_ref[...] + m * _rms_scale(m) * gpost_ref[...]


def mlp_residual(h, gpre, wup, wdn, gpost, *, name):
    t, d = h.shape
    f = wup.shape[1]
    tm, tf = MLP_TM, MLP_TF
    return pl.pallas_call(
        _mlp_kernel,
        grid=(t // tm, f // tf),
        in_specs=[pl.BlockSpec((tm, d), lambda i, j: (i, 0)),
                  pl.BlockSpec((1, d), lambda i, j: (0, 0)),
                  pl.BlockSpec((d, tf), lambda i, j: (0, j)),
                  pl.BlockSpec((tf, d), lambda i, j: (j, 0)),
                  pl.BlockSpec((1, d), lambda i, j: (0, 0))],
        out_specs=pl.BlockSpec((tm, d), lambda i, j: (i, 0)),
        out_shape=jax.ShapeDtypeStruct((t, d), F32),
        scratch_shapes=[pltpu.VMEM((tm, d), BF16), pltpu.VMEM((tm, d), F32)],
        compiler_params=_params(("parallel", "arbitrary")),
        name=name,
    )(h, gpre.reshape(1, d), wup, wdn, gpost.reshape(1, d))


def _retention_kernel(tab_ref, q_ref, k_ref, v_ref, g_ref, cos_ref, sin_ref, gnw_ref,
                      o_ref, state_ref, dmask_ref, xi_ref, zeta_ref, *, n_heads):
    head = pl.program_id(1)
    step = pl.program_id(2)
    blk = q_ref.shape[0]
    half = RET_HEAD_DK // 2
    log_gamma = tab_ref[head]
    gamma_blk = tab_ref[n_heads + head]

    @pl.when(step == 0)
    def _():
        state_ref[...] = jnp.zeros_like(state_ref)
        ri = lax.broadcasted_iota(jnp.int32, (blk, blk), 0)
        ci = lax.broadcasted_iota(jnp.int32, (blk, blk), 1)
        dist = jnp.abs(ri - ci).astype(F32)
        shift = int(math.log2(CHUNK))
        visible = jnp.right_shift(ci, shift) <= jnp.right_shift(ri, shift)
        dmask_ref[...] = jnp.where(visible, jnp.exp(dist * log_gamma), 0.0)
        pos = lax.broadcasted_iota(jnp.int32, (blk, V7X_LANES), 0).astype(F32)
        xi_ref[...] = jnp.exp((pos + 1.0) * log_gamma)
        zeta_ref[...] = jnp.exp((blk - 1.0 - pos) * log_gamma)

    cos = cos_ref[...]
    sin = sin_ref[...]

    def rotary(t):
        t1, t2 = t[:, :half], t[:, half:]
        return jnp.concatenate([t1 * cos - t2 * sin, t1 * sin + t2 * cos], axis=1)

    qr = rotary(q_ref[...].astype(F32))
    kr = rotary(k_ref[...].astype(F32)) * (RET_HEAD_DK ** -0.5)
    qb = qr.astype(BF16)
    kb = kr.astype(BF16)
    vb = v_ref[...].astype(BF16)

    scores = lax.dot_general(qb, kb, (((1,), (1,)), ((), ())), preferred_element_type=F32)
    inner = jnp.dot((scores * dmask_ref[...]).astype(BF16), vb, preferred_element_type=F32)
    state = state_ref[...]
    cross = jnp.dot(qb, state.astype(BF16), preferred_element_type=F32)
    xi = xi_ref[...]
    o = inner + cross * jnp.concatenate([xi] * (RET_HEAD_DV // V7X_LANES), axis=1)

    zeta = zeta_ref[...]
    kz = (kr * jnp.concatenate([zeta] * (RET_HEAD_DK // V7X_LANES), axis=1)).astype(BF16)
    upd = lax.dot_general(kz, vb, (((0,), (0,)), ((), ())), preferred_element_type=F32)
    state_ref[...] = state * gamma_blk + upd

    mu = jnp.mean(o, axis=-1, keepdims=True)
    dev = o - mu
    var = jnp.mean(dev * dev, axis=-1, keepdims=True)
    normed = dev * lax.rsqrt(var + GN_EPS) * gnw_ref[...]
    g = g_ref[...].astype(F32)
    o_ref[...] = (g * jax.nn.sigmoid(g) * normed).astype(o_ref.dtype)


def retention_core(proj, gn_w, *, batch, seq, name):
    t = proj.shape[0]
    n_heads = proj.shape[1] // (2 * RET_HEAD_DK + 2 * RET_HEAD_DV)
    blk = RET_L
    steps = seq // blk
    dv_per_dk = RET_HEAD_DV // RET_HEAD_DK
    k_off = n_heads
    v_off = 2 * n_heads // dv_per_dk
    g_off = v_off + n_heads

    half = RET_HEAD_DK // 2
    inv_freq = ROPE_BASE ** (-jnp.arange(half, dtype=F32) / half)
    ang = jnp.arange(seq).astype(F32)[:, None] * inv_freq[None, :]
    cos, sin = jnp.cos(ang), jnp.sin(ang)
    log_gamma = jnp.log1p(-jnp.exp2(-5.0 - jnp.arange(n_heads, dtype=F32)))
    tab = jnp.concatenate([log_gamma, jnp.exp(blk * log_gamma)])

    row = lambda b, h, c: b * steps + c
    return pl.pallas_call(
        functools.partial(_retention_kernel, n_heads=n_heads),
        grid=(batch, n_heads, steps),
        in_specs=[pl.BlockSpec(memory_space=pltpu.SMEM),
                  pl.BlockSpec((blk, RET_HEAD_DK), lambda b, h, c: (row(b, h, c), h)),
                  pl.BlockSpec((blk, RET_HEAD_DK), lambda b, h, c: (row(b, h, c), k_off + h)),
                  pl.BlockSpec((blk, RET_HEAD_DV), lambda b, h, c: (row(b, h, c), v_off + h)),
                  pl.BlockSpec((blk, RET_HEAD_DV), lambda b, h, c: (row(b, h, c), g_off + h)),
                  pl.BlockSpec((blk, half), lambda b, h, c: (c, 0)),
                  pl.BlockSpec((blk, half), lambda b, h, c: (c, 0)),
                  pl.BlockSpec((1, RET_HEAD_DV), lambda b, h, c: (0, h))],
        out_specs=pl.BlockSpec((blk, RET_HEAD_DV), lambda b, h, c: (row(b, h, c), h)),
        out_shape=jax.ShapeDtypeStruct((t, n_heads * RET_HEAD_DV), BF16),
        scratch_shapes=[pltpu.VMEM((RET_HEAD_DK, RET_HEAD_DV), F32),
                        pltpu.VMEM((blk, blk), F32),
                        pltpu.VMEM((blk, V7X_LANES), F32),
                        pltpu.VMEM((blk, V7X_LANES), F32)],
        compiler_params=_params(("parallel", "parallel", "arbitrary")),
        name=name,
    )(tab, proj, proj, proj, proj, cos, sin, gn_w.reshape(1, -1))


def _softplus(x):
    return jnp.maximum(x, 0.0) + jnp.log1p(jnp.exp(-jnp.abs(x)))


def _causal_conv_silu(raw, halo, w, b):
    row8 = lax.broadcasted_iota(jnp.int32, halo.shape, 0)
    out = b + raw * w[SSD_CONV_W - 1:SSD_CONV_W, :]
    for s in range(1, SSD_CONV_W):
        rolled = pltpu.roll(raw, s, axis=0)
        head = jnp.where(row8 < s, pltpu.roll(halo, s, axis=0), rolled[:V7X_SUBLANES])
        shifted = jnp.concatenate([head, rolled[V7X_SUBLANES:]], axis=0)
        out = out + shifted * w[SSD_CONV_W - 1 - s:SSD_CONV_W - s, :]
    return out * jax.nn.sigmoid(out)


def _ssd_kernel(z_ref, x_ref, b_ref, c_ref, wx_ref, wb_ref, wc_ref, bx_ref, bb_ref, bc_ref,
                dtc_ref, dtr_ref, prow_ref, pcol_ref, dexp_ref, nw_ref, o_ref,
                state_ref, hx_ref, hb_ref, hc_ref):
    step = pl.program_id(2)
    blk = x_ref.shape[0]

    @pl.when(step == 0)
    def _():
        state_ref[...] = jnp.zeros_like(state_ref)
        hx_ref[...] = jnp.zeros_like(hx_ref)
        hb_ref[...] = jnp.zeros_like(hb_ref)
        hc_ref[...] = jnp.zeros_like(hc_ref)

    x_raw = x_ref[...].astype(F32)
    b_raw = b_ref[...].astype(F32)
    c_raw = c_ref[...].astype(F32)
    xs = _causal_conv_silu(x_raw, hx_ref[...], wx_ref[...], bx_ref[...])
    bm = _causal_conv_silu(b_raw, hb_ref[...], wb_ref[...], bb_ref[...])
    cm = _causal_conv_silu(c_raw, hc_ref[...], wc_ref[...], bc_ref[...])
    hx_ref[...] = x_raw[blk - V7X_SUBLANES:, :]
    hb_ref[...] = b_raw[blk - V7X_SUBLANES:, :]
    hc_ref[...] = c_raw[blk - V7X_SUBLANES:, :]

    prow = prow_ref[...]
    pcol = pcol_ref[...]
    dt_c = _softplus(dtc_ref[...] + prow[0:1, :])
    dt_r = _softplus(dtr_ref[...] + pcol[:, 0:1])
    da_c = dt_c * (-jnp.exp(prow[1:2, :]))
    da_r = dt_r * (-jnp.exp(pcol[:, 1:2]))
    ri = lax.broadcasted_iota(jnp.int32, (blk, blk), 0)
    ci = lax.broadcasted_iota(jnp.int32, (blk, blk), 1)
    causal = ri >= ci
    acum_c = jnp.dot(causal.astype(F32), da_c, precision=lax.Precision.HIGHEST,
                     preferred_element_type=F32)
    acum_r = jnp.dot(da_r, (ri <= ci).astype(F32), precision=lax.Precision.HIGHEST,
                     preferred_element_type=F32)
    total = acum_c[blk - 1:blk, :]
    out_decay = jnp.exp(acum_c)
    in_weight = dt_c * jnp.exp(total - acum_c)
    total_decay = jnp.exp(total)

    cmb = cm.astype(BF16)
    bmb = bm.astype(BF16)
    cb = lax.dot_general(cmb, bmb, (((1,), (1,)), ((), ())), preferred_element_type=F32)
    state = state_ref[...]
    y_off = jnp.dot(cmb, state.astype(BF16), preferred_element_type=F32)

    lane = lax.broadcasted_iota(jnp.int32, (blk, V7X_LANES), 1)
    first = lane < SSD_HEADDIM
    first_row = first[0:1, :]
    dexp = dexp_ref[...]
    y_tiles, xw_tiles, decay_tiles = [], [], []
    for pair in range(SSD_GROUP_WIDTH // V7X_LANES):
        r0, r1 = 2 * pair, 2 * pair + 1
        cols = slice(pair * V7X_LANES, (pair + 1) * V7X_LANES)

        def expand(v, mask=first):
            return jnp.where(mask, v[:, r0:r0 + 1], v[:, r1:r1 + 1])

        xp = xs[:, cols]
        xdt = xp * expand(dt_c)
        y = y_off[:, cols] * expand(out_decay) + dexp[:, cols] * xp
        for r, keep in ((r0, first), (r1, jnp.logical_not(first))):
            seg = acum_c[:, r:r + 1] - acum_r[r:r + 1, :]
            lmat = jnp.exp(jnp.where(causal, seg, -jnp.inf))
            y = y + jnp.dot((cb * lmat).astype(BF16),
                            jnp.where(keep, xdt, 0.0).astype(BF16),
                            preferred_element_type=F32)
        y_tiles.append(y)
        xw_tiles.append((xp * expand(in_weight)).astype(BF16))
        decay_tiles.append(expand(total_decay, first_row))
    y = jnp.concatenate(y_tiles, axis=1)
    xw = jnp.concatenate(xw_tiles, axis=1)
    upd = lax.dot_general(bmb, xw, (((0,), (0,)), ((), ())), preferred_element_type=F32)
    state_ref[...] = state * jnp.concatenate(decay_tiles, axis=1) + upd

    z = z_ref[...].astype(F32)
    yg = y * (z * jax.nn.sigmoid(z))
    o_ref[...] = (yg * _rms_scale(yg) * nw_ref[...]).astype(o_ref.dtype)


def ssd_core(zxbc, dt_raw, conv_w, conv_b, dt_bias, a_log, d_skip, norm_w, *, batch, seq, name):
    t = zxbc.shape[0]
    g, r, n, gw = SSD_GROUPS, SSD_HEADS_PER_GROUP, SSD_STATE, SSD_GROUP_WIDTH
    d_inner = g * gw
    blk = SSD_L
    steps = seq // blk
    x_off = d_inner // gw
    b_off = 2 * d_inner // n
    c_off = b_off + g
    wb_off = d_inner // n
    wc_off = wb_off + g

    dt3 = dt_raw[:, :g * r].reshape(t, g, r)
    dt_c = dt3.transpose(1, 0, 2)
    dt_r = dt3.transpose(1, 2, 0)
    prm = jnp.stack([dt_bias, a_log]).reshape(2, g, r)
    prow = prm.transpose(1, 0, 2)
    pcol = prm.transpose(1, 2, 0)
    dexp = jnp.repeat(d_skip, SSD_HEADDIM).reshape(1, d_inner)
    conv_b = conv_b.reshape(1, -1)

    row = lambda b, gi, c: b * steps + c
    return pl.pallas_call(
        _ssd_kernel,
        grid=(batch, g, steps),
        in_specs=[pl.BlockSpec((blk, gw), lambda b, gi, c: (row(b, gi, c), gi)),
                  pl.BlockSpec((blk, gw), lambda b, gi, c: (row(b, gi, c), x_off + gi)),
                  pl.BlockSpec((blk, n), lambda b, gi, c: (row(b, gi, c), b_off + gi)),
                  pl.BlockSpec((blk, n), lambda b, gi, c: (row(b, gi, c), c_off + gi)),
                  pl.BlockSpec((SSD_CONV_W, gw), lambda b, gi, c: (0, gi)),
                  pl.BlockSpec((SSD_CONV_W, n), lambda b, gi, c: (0, wb_off + gi)),
                  pl.BlockSpec((SSD_CONV_W, n), lambda b, gi, c: (0, wc_off + gi)),
                  pl.BlockSpec((1, gw), lambda b, gi, c: (0, gi)),
                  pl.BlockSpec((1, n), lambda b, gi, c: (0, wb_off + gi)),
                  pl.BlockSpec((1, n), lambda b, gi, c: (0, wc_off + gi)),
                  pl.BlockSpec((None, blk, r), lambda b, gi, c: (gi, row(b, gi, c), 0)),
                  pl.BlockSpec((None, r, blk), lambda b, gi, c: (gi, 0, row(b, gi, c))),
                  pl.BlockSpec((None, 2, r), lambda b, gi, c: (gi, 0, 0)),
                  pl.BlockSpec((None, r, 2), lambda b, gi, c: (gi, 0, 0)),
                  pl.BlockSpec((1, gw), lambda b, gi, c: (0, gi)),
                  pl.BlockSpec((1, gw), lambda b, gi, c: (0, gi))],
        out_specs=pl.BlockSpec((blk, gw), lambda b, gi, c: (row(b, gi, c), gi)),
        out_shape=jax.ShapeDtypeStruct((t, d_inner), BF16),
        scratch_shapes=[pltpu.VMEM((n, gw), F32),
                        pltpu.VMEM((V7X_SUBLANES, gw), F32),
                        pltpu.VMEM((V7X_SUBLANES, n), F32),
                        pltpu.VMEM((V7X_SUBLANES, n), F32)],
        compiler_params=_params(("parallel", "parallel", "arbitrary")),
        name=name,
    )(zxbc, zxbc, zxbc, zxbc, conv_w, conv_w, conv_w, conv_b, conv_b, conv_b,
      dt_c, dt_r, prow, pcol, dexp, norm_w.reshape(1, d_inner))


def kernel(x, norm_mix_pre, norm_mix_post, norm_ffn_pre, norm_ffn_post, ret_w_in, ret_gn_w, ret_w_out, ssd_w_in, ssd_conv_w, ssd_conv_b, ssd_dt_bias, ssd_a_log, ssd_d, ssd_norm_w, ssd_w_out, mlp_w_up, mlp_w_down):
    batch, seq, d = x.shape
    h = x.reshape(batch * seq, d)
    act_dtype = F32

    proj = norm_matmul(h, norm_mix_pre[0], ret_w_in[0].astype(BF16),
                       out_dtype=act_dtype, name="ret_in_proj")
    y = retention_core(proj, ret_gn_w[0], batch=batch, seq=seq, name="retention_core")
    h = out_proj_residual(y, ret_w_out[0].astype(BF16), h, norm_mix_post[0], name="ret_out_proj")
    h = mlp_residual(h, norm_ffn_pre[0], mlp_w_up[0].astype(BF16), mlp_w_down[0].astype(BF16),
                     norm_ffn_post[0], name="mlp0")

    w_in = ssd_w_in[0]
    n_zxbc = w_in.shape[1] - SSD_GROUPS * SSD_HEADS_PER_GROUP
    zxbc, dt_raw = norm_matmul(h, norm_mix_pre[1], w_in[:, :n_zxbc].astype(BF16),
                               w_in[:, n_zxbc:].astype(BF16), out_dtype=act_dtype,
                               name="ssd_in_proj")
    y = ssd_core(zxbc, dt_raw, ssd_conv_w[0], ssd_conv_b[0], ssd_dt_bias[0], ssd_a_log[0],
                 ssd_d[0], ssd_norm_w[0], batch=batch, seq=seq, name="ssd_core")
    h = out_proj_residual(y, ssd_w_out[0].astype(BF16), h, norm_mix_post[1], name="ssd_out_proj")
    h = mlp_residual(h, norm_ffn_pre[1], mlp_w_up[1].astype(BF16), mlp_w_down[1].astype(BF16),
                     norm_ffn_post[1], name="mlp1")
    return h.reshape(batch, seq, d)
```

```python
import functools
import math

import jax
import jax.numpy as jnp
from jax import lax
from jax.experimental import pallas as pl
from jax.experimental.pallas import tpu as pltpu

F32 = jnp.float32
BF16 = jnp.bfloat16

RMS_EPS = 1e-6
GN_EPS = 1e-5
ROPE_BASE = 10000.0
CHUNK = 64
LOG2_E = math.log2(math.e)

RET_HEAD_DK = 256
RET_HEAD_DV = 512
SSD_HEADDIM = 64
SSD_GROUPS = 8
SSD_HEADS_PER_GROUP = 8
SSD_STATE = 128
SSD_CONV_W = 4
SSD_GROUP_WIDTH = SSD_HEADS_PER_GROUP * SSD_HEADDIM

V7X_LANES = 128
V7X_SUBLANES = 8
V7X_VMEM_LIMIT_BYTES = 56 * 1024 * 1024

PROJ_TM, PROJ_TN = 1024, 1024
OUT_TM = 512
MLP_TM, MLP_TF = 512, 1024
RET_STEP, RET_BLK = 512, 256
SSD_DECAY_STEP = 2048
SSD_STEP, SSD_BLK = 1024, 128


def _params(semantics):
    return pltpu.CompilerParams(dimension_semantics=semantics,
                                vmem_limit_bytes=V7X_VMEM_LIMIT_BYTES)


def _rms_scale(x):
    return lax.rsqrt(jnp.mean(x * x, axis=-1, keepdims=True) + RMS_EPS)


def _silu(x):
    return x * jax.nn.sigmoid(x)


def _norm_matmul_kernel(x_ref, g_ref, w_ref, o_ref, u_ref):
    @pl.when(pl.program_id(1) == 0)
    def _():
        x = x_ref[...]
        u_ref[...] = (x * _rms_scale(x) * g_ref[...]).astype(BF16)

    o_ref[...] = jnp.dot(u_ref[...], w_ref[...],
                         preferred_element_type=F32).astype(o_ref.dtype)


def _norm_matmul2_kernel(x_ref, g_ref, w_ref, w2_ref, o_ref, o2_ref, u_ref):
    @pl.when(pl.program_id(1) == 0)
    def _():
        x = x_ref[...]
        u = (x * _rms_scale(x) * g_ref[...]).astype(BF16)
        u_ref[...] = u
        o2_ref[...] = jnp.dot(u, w2_ref[...], preferred_element_type=F32)

    o_ref[...] = jnp.dot(u_ref[...], w_ref[...],
                         preferred_element_type=F32).astype(o_ref.dtype)


def norm_matmul(h, g, w, w2=None, *, out_dtype, name):
    t, d = h.shape
    n = w.shape[1]
    tm, tn = PROJ_TM, PROJ_TN
    in_specs = [pl.BlockSpec((tm, d), lambda i, j: (i, 0)),
                pl.BlockSpec((1, d), lambda i, j: (0, 0)),
                pl.BlockSpec((d, tn), lambda i, j: (0, j))]
    out_specs = pl.BlockSpec((tm, tn), lambda i, j: (i, j))
    out_shape = jax.ShapeDtypeStruct((t, n), out_dtype)
    args = [h, g.reshape(1, d), w]
    body = _norm_matmul_kernel
    if w2 is not None:
        n2 = w2.shape[1]
        in_specs.append(pl.BlockSpec((d, n2), lambda i, j: (0, 0)))
        out_specs = [out_specs, pl.BlockSpec((tm, n2), lambda i, j: (i, 0))]
        out_shape = [out_shape, jax.ShapeDtypeStruct((t, n2), F32)]
        args.append(w2)
        body = _norm_matmul2_kernel
    return pl.pallas_call(
        body,
        grid=(t // tm, n // tn),
        in_specs=in_specs,
        out_specs=out_specs,
        out_shape=out_shape,
        scratch_shapes=[pltpu.VMEM((tm, d), BF16)],
        compiler_params=_params(("parallel", "arbitrary")),
        name=name,
    )(*args)


def _out_proj_kernel(y_ref, w_ref, h_ref, g_ref, o_ref):
    m = jnp.dot(y_ref[...], w_ref[...], preferred_element_type=F32)
    o_ref[...] = h_ref[...] + m * _rms_scale(m) * g_ref[...]


def out_proj_residual(y, w, h, g, *, name):
    t, k = y.shape
    d = w.shape[1]
    tm = OUT_TM
    return pl.pallas_call(
        _out_proj_kernel,
        grid=(t // tm,),
        in_specs=[pl.BlockSpec((tm, k), lambda i: (i, 0)),
                  pl.BlockSpec((k, d), lambda i: (0, 0), pipeline_mode=pl.Buffered(1)),
                  pl.BlockSpec((tm, d), lambda i: (i, 0)),
                  pl.BlockSpec((1, d), lambda i: (0, 0))],
        out_specs=pl.BlockSpec((tm, d), lambda i: (i, 0)),
        out_shape=jax.ShapeDtypeStruct((t, d), F32),
        compiler_params=_params(("parallel",)),
        name=name,
    )(y, w, h, g.reshape(1, d))


def _mlp_kernel(x_ref, gpre_ref, wup_ref, wdn_ref, gpost_ref, o_ref, u_ref, acc_ref):
    j = pl.program_id(1)

    @pl.when(j == 0)
    def _():
        x = x_ref[...]
        u_ref[...] = (x * _rms_scale(x) * gpre_ref[...]).astype(BF16)
        acc_ref[...] = jnp.zeros_like(acc_ref)

    hid = jnp.maximum(jnp.dot(u_ref[...], wup_ref[...], preferred_element_type=F32), 0.0)
    acc_ref[...] += jnp.dot((hid * hid).astype(BF16), wdn_ref[...],
                            preferred_element_type=F32)

    @pl.when(j == pl.num_programs(1) - 1)
    def _():
        m = acc_ref[...]
        o_ref[...] = x_ref[...] + m * _rms_scale(m) * gpost_ref[...]


def mlp_residual(h, gpre, wup, wdn, gpost, *, name):
    t, d = h.shape
    f = wup.shape[1]
    tm, tf = MLP_TM, MLP_TF
    return pl.pallas_call(
        _mlp_kernel,
        grid=(t // tm, f // tf),
        in_specs=[pl.BlockSpec((tm, d), lambda i, j: (i, 0)),
                  pl.BlockSpec((1, d), lambda i, j: (0, 0)),
                  pl.BlockSpec((d, tf), lambda i, j: (0, j)),
                  pl.BlockSpec((tf, d), lambda i, j: (j, 0)),
                  pl.BlockSpec((1, d), lambda i, j: (0, 0))],
        out_specs=pl.BlockSpec((tm, d), lambda i, j: (i, 0)),
        out_shape=jax.ShapeDtypeStruct((t, d), F32),
        scratch_shapes=[pltpu.VMEM((tm, d), BF16), pltpu.VMEM((tm, d), F32)],
        compiler_params=_params(("parallel", "arbitrary")),
        name=name,
    )(h, gpre.reshape(1, d), wup, wdn, gpost.reshape(1, d))


def _retention_kernel(tab_ref, q_ref, k_ref, v_ref, g_ref, cos_ref, sin_ref, gnw_ref,
                      o_ref, state_ref, dmask_ref, xi_ref, zeta_ref, *, n_heads, blk):
    head = pl.program_id(1)
    half = RET_HEAD_DK // 2
    log_gamma = tab_ref[head]
    gamma_blk = tab_ref[n_heads + head]

    @pl.when(pl.program_id(2) == 0)
    def _():
        state_ref[...] = jnp.zeros_like(state_ref)
        ri = lax.broadcasted_iota(jnp.int32, (blk, blk), 0)
        ci = lax.broadcasted_iota(jnp.int32, (blk, blk), 1)
        dist = jnp.abs(ri - ci).astype(F32)
        shift = int(math.log2(CHUNK))
        visible = jnp.right_shift(ci, shift) <= jnp.right_shift(ri, shift)
        dmask_ref[...] = jnp.where(visible, jnp.exp(dist * log_gamma), 0.0)
        pos = lax.broadcasted_iota(jnp.int32, (blk, V7X_LANES), 0).astype(F32)
        xi_ref[...] = jnp.exp((pos + 1.0) * log_gamma)
        zeta_ref[...] = jnp.exp((blk - 1.0 - pos) * log_gamma)

    xi = jnp.concatenate([xi_ref[...]] * (RET_HEAD_DV // V7X_LANES), axis=1)
    zeta = jnp.concatenate([zeta_ref[...]] * (RET_HEAD_DK // V7X_LANES), axis=1)
    gnw = gnw_ref[...]

    for sub in range(q_ref.shape[0] // blk):
        rows = pl.ds(sub * blk, blk)
        cos = cos_ref[rows, :]
        sin = sin_ref[rows, :]

        def rotary(t):
            t1, t2 = t[:, :half], t[:, half:]
            return jnp.concatenate([t1 * cos - t2 * sin, t1 * sin + t2 * cos], axis=1)

        qr = rotary(q_ref[rows, :].astype(F32))
        kr = rotary(k_ref[rows, :].astype(F32)) * (RET_HEAD_DK ** -0.5)
        qb = qr.astype(BF16)
        kb = kr.astype(BF16)
        vb = v_ref[rows, :].astype(BF16)

        scores = lax.dot_general(qb, kb, (((1,), (1,)), ((), ())), preferred_element_type=F32)
        inner = jnp.dot((scores * dmask_ref[...]).astype(BF16), vb, preferred_element_type=F32)
        state = state_ref[...]
        cross = jnp.dot(qb, state.astype(BF16), preferred_element_type=F32)
        o = inner + cross * xi

        kz = (kr * zeta).astype(BF16)
        upd = lax.dot_general(kz, vb, (((0,), (0,)), ((), ())), preferred_element_type=F32)
        state_ref[...] = state * gamma_blk + upd

        mu = jnp.mean(o, axis=-1, keepdims=True)
        dev = o - mu
        var = jnp.mean(dev * dev, axis=-1, keepdims=True)
        normed = dev * lax.rsqrt(var + GN_EPS) * gnw
        o_ref[rows, :] = (_silu(g_ref[rows, :].astype(F32)) * normed).astype(o_ref.dtype)


def retention_core(proj, gn_w, *, batch, seq, name):
    t = proj.shape[0]
    n_heads = proj.shape[1] // (2 * RET_HEAD_DK + 2 * RET_HEAD_DV)
    step, blk = RET_STEP, RET_BLK
    steps = seq // step
    dv_per_dk = RET_HEAD_DV // RET_HEAD_DK
    k_off = n_heads
    v_off = 2 * n_heads // dv_per_dk
    g_off = v_off + n_heads

    half = RET_HEAD_DK // 2
    inv_freq = ROPE_BASE ** (-jnp.arange(half, dtype=F32) / half)
    ang = jnp.arange(seq).astype(F32)[:, None] * inv_freq[None, :]
    cos, sin = jnp.cos(ang), jnp.sin(ang)
    log_gamma = jnp.log1p(-jnp.exp2(-5.0 - jnp.arange(n_heads, dtype=F32)))
    tab = jnp.concatenate([log_gamma, jnp.exp(blk * log_gamma)])

    row = lambda b, h, c: b * steps + c
    return pl.pallas_call(
        functools.partial(_retention_kernel, n_heads=n_heads, blk=blk),
        grid=(batch, n_heads, steps),
        in_specs=[pl.BlockSpec(memory_space=pltpu.SMEM),
                  pl.BlockSpec((step, RET_HEAD_DK), lambda b, h, c: (row(b, h, c), h)),
                  pl.BlockSpec((step, RET_HEAD_DK), lambda b, h, c: (row(b, h, c), k_off + h)),
                  pl.BlockSpec((step, RET_HEAD_DV), lambda b, h, c: (row(b, h, c), v_off + h)),
                  pl.BlockSpec((step, RET_HEAD_DV), lambda b, h, c: (row(b, h, c), g_off + h)),
                  pl.BlockSpec((step, half), lambda b, h, c: (c, 0)),
                  pl.BlockSpec((step, half), lambda b, h, c: (c, 0)),
                  pl.BlockSpec((1, RET_HEAD_DV), lambda b, h, c: (0, h))],
        out_specs=pl.BlockSpec((step, RET_HEAD_DV), lambda b, h, c: (row(b, h, c), h)),
        out_shape=jax.ShapeDtypeStruct((t, n_heads * RET_HEAD_DV), BF16),
        scratch_shapes=[pltpu.VMEM((RET_HEAD_DK, RET_HEAD_DV), F32),
                        pltpu.VMEM((blk, blk), F32),
                        pltpu.VMEM((blk, V7X_LANES), F32),
                        pltpu.VMEM((blk, V7X_LANES), F32)],
        compiler_params=_params(("parallel", "parallel", "arbitrary")),
        name=name,
    )(tab, proj, proj, proj, proj, cos, sin, gn_w.reshape(1, -1))


def _softplus(x):
    return jnp.maximum(x, 0.0) + jnp.log1p(jnp.exp(-jnp.abs(x)))


def _split_bf16(v, terms):
    parts = []
    for _ in range(terms - 1):
        part = v.astype(BF16)
        parts.append(part)
        v = v - part.astype(F32)
    parts.append(v.astype(BF16))
    return jnp.concatenate(parts, axis=0)


def _sum_terms(stacked, rows):
    out = stacked[0:rows]
    for i in range(1, stacked.shape[0] // rows):
        out = out + stacked[i * rows:(i + 1) * rows]
    return out


def _ssd_decay_kernel(dt_ref, prm_ref, dt_out_ref, acum_ref, inw_ref, *, blk):
    prm = prm_ref[...]
    a2 = -jnp.exp(prm[:, 1:2]) * LOG2_E
    ri = lax.broadcasted_iota(jnp.int32, (blk, blk), 0)
    ci = lax.broadcasted_iota(jnp.int32, (blk, blk), 1)
    upper_ones = jnp.where(ri <= ci, 1.0, 0.0).astype(BF16)
    n_heads = dt_ref.shape[0]
    for sub in range(dt_ref.shape[1] // blk):
        cols = pl.ds(sub * blk, blk)
        dt = _softplus(dt_ref[:, cols] + prm[:, 0:1])
        acum = _sum_terms(jnp.dot(_split_bf16(dt * a2, 3), upper_ones,
                                  preferred_element_type=F32), n_heads)
        dt_out_ref[:, cols] = dt
        acum_ref[:, cols] = acum
        inw_ref[:, cols] = dt * jnp.exp2(acum[:, blk - 1:blk] - acum)


def ssd_decays(dt_raw_t, dt_bias, a_log, *, blk, name):
    n_heads, t = dt_raw_t.shape
    step = SSD_DECAY_STEP
    spec = pl.BlockSpec((n_heads, step), lambda i: (0, i))
    shape = jax.ShapeDtypeStruct((n_heads, t), F32)
    return pl.pallas_call(
        functools.partial(_ssd_decay_kernel, blk=blk),
        grid=(t // step,),
        in_specs=[spec, pl.BlockSpec((n_heads, 2), lambda i: (0, 0))],
        out_specs=[spec, spec, spec],
        out_shape=[shape, shape, shape],
        compiler_params=_params(("parallel",)),
        name=name,
    )(dt_raw_t, jnp.stack([dt_bias, a_log], axis=-1))


def _ssd_kernel(z_ref, x_ref, b_ref, c_ref, wx_ref, wb_ref, wc_ref, bx_ref, bb_ref, bc_ref,
                dt_ref, acum_ref, inw_ref, dexp_ref, nw_ref, o_ref,
                state_ref, xe_ref, be_ref, ce_ref, *, blk):
    step_rows = x_ref.shape[0]
    halo = V7X_SUBLANES

    @pl.when(pl.program_id(2) == 0)
    def _():
        state_ref[...] = jnp.zeros_like(state_ref)
        xe_ref[0:halo, :] = jnp.zeros((halo, xe_ref.shape[1]), F32)
        be_ref[0:halo, :] = jnp.zeros((halo, be_ref.shape[1]), F32)
        ce_ref[0:halo, :] = jnp.zeros((halo, ce_ref.shape[1]), F32)

    xe_ref[halo:, :] = x_ref[...].astype(F32)
    be_ref[halo:, :] = b_ref[...].astype(F32)
    ce_ref[halo:, :] = c_ref[...].astype(F32)

    def conv_silu(ext_ref, w_ref, bias_ref, off):
        out = bias_ref[...] + ext_ref[pl.ds(halo + off, blk), :] * w_ref[SSD_CONV_W - 1:SSD_CONV_W, :]
        for s in range(1, SSD_CONV_W):
            out = out + (ext_ref[pl.ds(halo + off - s, blk), :]
                         * w_ref[SSD_CONV_W - 1 - s:SSD_CONV_W - s, :])
        return _silu(out)

    ri = lax.broadcasted_iota(jnp.int32, (blk, blk), 0)
    ci = lax.broadcasted_iota(jnp.int32, (blk, blk), 1)
    causal = ri >= ci
    lane = lax.broadcasted_iota(jnp.int32, (blk, V7X_LANES), 1)
    first = lane < SSD_HEADDIM
    second = jnp.logical_not(first)
    dexp = dexp_ref[...]
    nw = nw_ref[...]
    n_heads = dt_ref.shape[0]
    gw = x_ref.shape[1]

    def head_selector(terms, lanes_per_head):
        k = lax.broadcasted_iota(jnp.int32, (terms * n_heads, n_heads * lanes_per_head), 0)
        n = lax.broadcasted_iota(jnp.int32, (terms * n_heads, n_heads * lanes_per_head), 1)
        shift = int(math.log2(lanes_per_head))
        return jnp.where((k & (n_heads - 1)) == jnp.right_shift(n, shift), 1.0, 0.0).astype(BF16)

    sel_mask = head_selector(3, blk)
    sel_chan = head_selector(2, SSD_HEADDIM)
    transposed_lhs = (((0,), (0,)), ((), ()))

    for sub in range(step_rows // blk):
        off = sub * blk
        dt_r = dt_ref[:, pl.ds(off, blk)]
        acum_r = acum_ref[:, pl.ds(off, blk)]
        acum_bc = lax.dot_general(_split_bf16(acum_r, 3), sel_mask, transposed_lhs,
                                  preferred_element_type=F32)
        in_weight = lax.dot_general(_split_bf16(inw_ref[:, pl.ds(off, blk)], 2), sel_chan,
                                    transposed_lhs, preferred_element_type=F32)
        xs = conv_silu(xe_ref, wx_ref, bx_ref, off)
        bmb = conv_silu(be_ref, wb_ref, bb_ref, off).astype(BF16)
        cmb = conv_silu(ce_ref, wc_ref, bc_ref, off).astype(BF16)

        cb = lax.dot_general(cmb, bmb, (((1,), (1,)), ((), ())), preferred_element_type=F32)
        state = state_ref[...]
        y_off = jnp.dot(cmb, state.astype(BF16), preferred_element_type=F32)

        y_tiles, decay_tiles = [], []
        for pair in range(gw // V7X_LANES):
            r0, r1 = 2 * pair, 2 * pair + 1
            cols = slice(pair * V7X_LANES, (pair + 1) * V7X_LANES)
            xp = xs[:, cols]
            out_decay = jnp.exp2(jnp.where(first, acum_bc[:, r0 * blk:r0 * blk + V7X_LANES],
                                           acum_bc[:, r1 * blk:r1 * blk + V7X_LANES]))
            y = y_off[:, cols] * out_decay + dexp[:, cols] * xp
            for r, keep in ((r0, first), (r1, second)):
                seg = acum_bc[:, r * blk:(r + 1) * blk] - acum_r[r:r + 1, :]
                lmat = jnp.exp2(jnp.where(causal, seg, -jnp.inf))
                y = y + jnp.dot((cb * lmat * dt_r[r:r + 1, :]).astype(BF16),
                                jnp.where(keep, xp, 0.0).astype(BF16),
                                preferred_element_type=F32)
            y_tiles.append(y)
            decay_tiles.append(out_decay[blk - 1:blk, :])
        y = jnp.concatenate(y_tiles, axis=1)
        xw = (xs * in_weight).astype(BF16)
        upd = lax.dot_general(bmb, xw, transposed_lhs, preferred_element_type=F32)
        state_ref[...] = state * jnp.concatenate(decay_tiles, axis=1) + upd

        yg = y * _silu(z_ref[pl.ds(off, blk), :].astype(F32))
        o_ref[pl.ds(off, blk), :] = (yg * _rms_scale(yg) * nw).astype(o_ref.dtype)

    xe_ref[0:halo, :] = xe_ref[step_rows:step_rows + halo, :]
    be_ref[0:halo, :] = be_ref[step_rows:step_rows + halo, :]
    ce_ref[0:halo, :] = ce_ref[step_rows:step_rows + halo, :]


def ssd_core(zxbc, dt_raw, conv_w, conv_b, dt_bias, a_log, d_skip, norm_w, *, batch, seq, name):
    t = zxbc.shape[0]
    g, r, n, gw = SSD_GROUPS, SSD_HEADS_PER_GROUP, SSD_STATE, SSD_GROUP_WIDTH
    d_inner = g * gw
    step, blk = SSD_STEP, SSD_BLK
    steps = seq // step
    x_off = d_inner // gw
    b_off = 2 * d_inner // n
    c_off = b_off + g
    wb_off = d_inner // n
    wc_off = wb_off + g

    dt_r, acum_r, inw_r = (a.reshape(g, r, t) for a in
                           ssd_decays(dt_raw.T, dt_bias, a_log, blk=blk, name=name + "_decays"))
    dexp = jnp.repeat(d_skip, SSD_HEADDIM).reshape(1, d_inner)
    conv_b = conv_b.reshape(1, -1)

    row = lambda b, gi, c: b * steps + c
    return pl.pallas_call(
        functools.partial(_ssd_kernel, blk=blk),
        grid=(batch, g, steps),
        in_specs=[pl.BlockSpec((step, gw), lambda b, gi, c: (row(b, gi, c), gi)),
                  pl.BlockSpec((step, gw), lambda b, gi, c: (row(b, gi, c), x_off + gi)),
                  pl.BlockSpec((step, n), lambda b, gi, c: (row(b, gi, c), b_off + gi)),
                  pl.BlockSpec((step, n), lambda b, gi, c: (row(b, gi, c), c_off + gi)),
                  pl.BlockSpec((SSD_CONV_W, gw), lambda b, gi, c: (0, gi)),
                  pl.BlockSpec((SSD_CONV_W, n), lambda b, gi, c: (0, wb_off + gi)),
                  pl.BlockSpec((SSD_CONV_W, n), lambda b, gi, c: (0, wc_off + gi)),
                  pl.BlockSpec((1, gw), lambda b, gi, c: (0, gi)),
                  pl.BlockSpec((1, n), lambda b, gi, c: (0, wb_off + gi)),
                  pl.BlockSpec((1, n), lambda b, gi, c: (0, wc_off + gi)),
                  pl.BlockSpec((None, r, step), lambda b, gi, c: (gi, 0, row(b, gi, c))),
                  pl.BlockSpec((None, r, step), lambda b, gi, c: (gi, 0, row(b, gi, c))),
                  pl.BlockSpec((None, r, step), lambda b, gi, c: (gi, 0, row(b, gi, c))),
                  pl.BlockSpec((1, gw), lambda b, gi, c: (0, gi)),
                  pl.BlockSpec((1, gw), lambda b, gi, c: (0, gi))],
        out_specs=pl.BlockSpec((step, gw), lambda b, gi, c: (row(b, gi, c), gi)),
        out_shape=jax.ShapeDtypeStruct((t, d_inner), BF16),
        scratch_shapes=[pltpu.VMEM((n, gw), F32),
                        pltpu.VMEM((step + V7X_SUBLANES, gw), F32),
                        pltpu.VMEM((step + V7X_SUBLANES, n), F32),
                        pltpu.VMEM((step + V7X_SUBLANES, n), F32)],
        compiler_params=_params(("parallel", "parallel", "arbitrary")),
        name=name,
    )(zxbc, zxbc, zxbc, zxbc, conv_w, conv_w, conv_w, conv_b, conv_b, conv_b,
      dt_r, acum_r, inw_r, dexp, norm_w.reshape(1, d_inner))


def kernel(x, norm_mix_pre, norm_mix_post, norm_ffn_pre, norm_ffn_post, ret_w_in, ret_gn_w, ret_w_out, ssd_w_in, ssd_conv_w, ssd_conv_b, ssd_dt_bias, ssd_a_log, ssd_d, ssd_norm_w, ssd_w_out, mlp_w_up, mlp_w_down):
    batch, seq, d = x.shape
    h = x.reshape(batch * seq, d)
    act_dtype = F32

    proj = norm_matmul(h, norm_mix_pre[0], ret_w_in[0].astype(BF16),
                       out_dtype=act_dtype, name="ret_in_proj")
    y = retention_core(proj, ret_gn_w[0], batch=batch, seq=seq, name="retention_core")
    h = out_proj_residual(y, ret_w_out[0].astype(BF16), h, norm_mix_post[0], name="ret_out_proj")
    h = mlp_residual(h, norm_ffn_pre[0], mlp_w_up[0].astype(BF16), mlp_w_down[0].astype(BF16),
                     norm_ffn_post[0], name="mlp0")

    w_in = ssd_w_in[0]
    n_zxbc = w_in.shape[1] - SSD_GROUPS * SSD_HEADS_PER_GROUP
    zxbc, dt_raw = norm_matmul(h, norm_mix_pre[1], w_in[:, :n_zxbc].astype(BF16),
                               w_in[:, n_zxbc:].astype(BF16), out_dtype=act_dtype,
                               name="ssd_in_proj")
    y = ssd_core(zxbc, dt_raw, ssd_conv_w[0], ssd_conv_b[0], ssd_dt_bias[0], ssd_a_log[0],
                 ssd_d[0], ssd_norm_w[0], batch=batch, seq=seq, name="ssd_core")
    h = out_proj_residual(y, ssd_w_out[0].astype(BF16), h, norm_mix_post[1], name="ssd_out_proj")
    h = mlp_residual(h, norm_ffn_pre[1], mlp_w_up[1].astype(BF16), mlp_w_down[1].astype(BF16),
                     norm_ffn_post[1], name="mlp1")
    return h.reshape(batch, seq, d)
```

```python
import functools
import math

import jax
import jax.numpy as jnp
from jax import lax
from jax.experimental import pallas as pl
from jax.experimental.pallas import tpu as pltpu

F32 = jnp.float32
BF16 = jnp.bfloat16

RMS_EPS = 1e-6
GN_EPS = 1e-5
ROPE_BASE = 10000.0
CHUNK = 64
LOG2_E = math.log2(math.e)

RET_HEAD_DK = 256
RET_HEAD_DV = 512
SSD_HEADDIM = 64
SSD_GROUPS = 8
SSD_HEADS_PER_GROUP = 8
SSD_STATE = 128
SSD_CONV_W = 4
SSD_GROUP_WIDTH = SSD_HEADS_PER_GROUP * SSD_HEADDIM

V7X_LANES = 128
V7X_SUBLANES = 8
V7X_VMEM_LIMIT_BYTES = 56 * 1024 * 1024

PROJ_TM, PROJ_TN = 1024, 1024
OUT_TM = 512
MLP_TM, MLP_TF = 512, 1024
RET_STEP, RET_BLK = 1024, 256
SSD_DECAY_STEP = 2048
SSD_STEP, SSD_BLK = 1024, 128


def _params(semantics):
    return pltpu.CompilerParams(dimension_semantics=semantics,
                                vmem_limit_bytes=V7X_VMEM_LIMIT_BYTES)


def _rms_scale(x):
    return lax.rsqrt(jnp.mean(x * x, axis=-1, keepdims=True) + RMS_EPS)


def _silu(x):
    half = 0.5 * x
    return half + half * jnp.tanh(half)


def _norm_matmul_kernel(x_ref, g_ref, w_ref, o_ref, u_ref):
    @pl.when(pl.program_id(1) == 0)
    def _():
        x = x_ref[...]
        u_ref[...] = (x * _rms_scale(x) * g_ref[...]).astype(BF16)

    o_ref[...] = jnp.dot(u_ref[...], w_ref[...],
                         preferred_element_type=F32).astype(o_ref.dtype)


def _norm_matmul2_kernel(x_ref, g_ref, w_ref, w2_ref, o_ref, o2_ref, u_ref):
    @pl.when(pl.program_id(1) == 0)
    def _():
        x = x_ref[...]
        u = (x * _rms_scale(x) * g_ref[...]).astype(BF16)
        u_ref[...] = u
        o2_ref[...] = jnp.dot(u, w2_ref[...], preferred_element_type=F32)

    o_ref[...] = jnp.dot(u_ref[...], w_ref[...],
                         preferred_element_type=F32).astype(o_ref.dtype)


def _gain_spec(d, layer):
    return pl.BlockSpec((None, 1, d), lambda *_: (layer, 0, 0))


def _gains(g):
    return g.reshape(g.shape[0], 1, g.shape[1])


def norm_matmul(h, g, layer, w, w_layer, *, n_out, w_extra=None, out_dtype, name):
    t, d = h.shape
    tm, tn = PROJ_TM, PROJ_TN
    in_specs = [pl.BlockSpec((tm, d), lambda i, j: (i, 0)),
                _gain_spec(d, layer),
                pl.BlockSpec((None, d, tn), lambda i, j: (w_layer, 0, j))]
    out_specs = pl.BlockSpec((tm, tn), lambda i, j: (i, j))
    out_shape = jax.ShapeDtypeStruct((t, n_out), out_dtype)
    args = [h, _gains(g), w]
    body = _norm_matmul_kernel
    if w_extra is not None:
        n_extra = w_extra.shape[1]
        in_specs.append(pl.BlockSpec((d, n_extra), lambda i, j: (0, 0)))
        out_specs = [out_specs, pl.BlockSpec((tm, n_extra), lambda i, j: (i, 0))]
        out_shape = [out_shape, jax.ShapeDtypeStruct((t, n_extra), F32)]
        args.append(w_extra)
        body = _norm_matmul2_kernel
    n = n_out
    return pl.pallas_call(
        body,
        grid=(t // tm, n // tn),
        in_specs=in_specs,
        out_specs=out_specs,
        out_shape=out_shape,
        scratch_shapes=[pltpu.VMEM((tm, d), BF16)],
        compiler_params=_params(("parallel", "arbitrary")),
        name=name,
    )(*args)


def _out_proj_kernel(y_ref, w_ref, h_ref, g_ref, o_ref):
    m = jnp.dot(y_ref[...], w_ref[...], preferred_element_type=F32)
    o_ref[...] = h_ref[...] + m * _rms_scale(m) * g_ref[...]


def out_proj_residual(y, w, w_layer, h, g, layer, *, name):
    t, k = y.shape
    d = w.shape[2]
    tm = OUT_TM
    return pl.pallas_call(
        _out_proj_kernel,
        grid=(t // tm,),
        in_specs=[pl.BlockSpec((tm, k), lambda i: (i, 0)),
                  pl.BlockSpec((None, k, d), lambda i: (w_layer, 0, 0),
                               pipeline_mode=pl.Buffered(1)),
                  pl.BlockSpec((tm, d), lambda i: (i, 0)),
                  _gain_spec(d, layer)],
        out_specs=pl.BlockSpec((tm, d), lambda i: (i, 0)),
        out_shape=jax.ShapeDtypeStruct((t, d), F32),
        compiler_params=_params(("parallel",)),
        name=name,
    )(y, w, h, _gains(g))


def _mlp_kernel(x_ref, gpre_ref, wup_ref, wdn_ref, gpost_ref, o_ref, u_ref, acc_ref):
    j = pl.program_id(1)

    @pl.when(j == 0)
    def _():
        x = x_ref[...]
        u_ref[...] = (x * _rms_scale(x) * gpre_ref[...]).astype(BF16)
        acc_ref[...] = jnp.zeros_like(acc_ref)

    hid = jnp.maximum(jnp.dot(u_ref[...], wup_ref[...], preferred_element_type=F32), 0.0)
    acc_ref[...] += jnp.dot((hid * hid).astype(BF16), wdn_ref[...],
                            preferred_element_type=F32)

    @pl.when(j == pl.num_programs(1) - 1)
    def _():
        m = acc_ref[...]
        o_ref[...] = x_ref[...] + m * _rms_scale(m) * gpost_ref[...]


def mlp_residual(h, gpre, wup, wdn, gpost, layer, *, name):
    t, d = h.shape
    f = wup.shape[2]
    tm, tf = MLP_TM, MLP_TF
    return pl.pallas_call(
        _mlp_kernel,
        grid=(t // tm, f // tf),
        in_specs=[pl.BlockSpec((tm, d), lambda i, j: (i, 0)),
                  _gain_spec(d, layer),
                  pl.BlockSpec((None, d, tf), lambda i, j: (layer, 0, j)),
                  pl.BlockSpec((None, tf, d), lambda i, j: (layer, j, 0)),
                  _gain_spec(d, layer)],
        out_specs=pl.BlockSpec((tm, d), lambda i, j: (i, 0)),
        out_shape=jax.ShapeDtypeStruct((t, d), F32),
        scratch_shapes=[pltpu.VMEM((tm, d), BF16), pltpu.VMEM((tm, d), F32)],
        compiler_params=_params(("parallel", "arbitrary")),
        name=name,
    )(h, _gains(gpre), wup, wdn, _gains(gpost))


def _retention_kernel(tab_ref, q_ref, k_ref, v_ref, g_ref, cos_ref, sin_ref, gnw_ref,
                      o_ref, state_ref, dmask_ref, xi_ref, zeta_ref, *, n_heads, blk):
    head = pl.program_id(1)
    half = RET_HEAD_DK // 2
    log_gamma = tab_ref[head]
    gamma_blk = tab_ref[n_heads + head]

    @pl.when(pl.program_id(2) == 0)
    def _():
        state_ref[...] = jnp.zeros_like(state_ref)
        ri = lax.broadcasted_iota(jnp.int32, (blk, blk), 0)
        ci = lax.broadcasted_iota(jnp.int32, (blk, blk), 1)
        dist = jnp.abs(ri - ci).astype(F32)
        shift = int(math.log2(CHUNK))
        visible = jnp.right_shift(ci, shift) <= jnp.right_shift(ri, shift)
        dmask_ref[...] = jnp.where(visible, jnp.exp(dist * log_gamma), 0.0)
        pos = lax.broadcasted_iota(jnp.int32, (blk, V7X_LANES), 0).astype(F32)
        xi_ref[...] = jnp.exp((pos + 1.0) * log_gamma)
        zeta_ref[...] = jnp.exp((blk - 1.0 - pos) * log_gamma)

    xi = jnp.concatenate([xi_ref[...]] * (RET_HEAD_DV // V7X_LANES), axis=1)
    zeta = jnp.concatenate([zeta_ref[...]] * (RET_HEAD_DK // V7X_LANES), axis=1)
    gnw = gnw_ref[...]

    for sub in range(q_ref.shape[0] // blk):
        rows = pl.ds(sub * blk, blk)
        cos = cos_ref[rows, :]
        sin = sin_ref[rows, :]

        def rotary(t):
            t1, t2 = t[:, :half], t[:, half:]
            return jnp.concatenate([t1 * cos - t2 * sin, t1 * sin + t2 * cos], axis=1)

        qr = rotary(q_ref[rows, :].astype(F32))
        kr = rotary(k_ref[rows, :].astype(F32)) * (RET_HEAD_DK ** -0.5)
        qb = qr.astype(BF16)
        kb = kr.astype(BF16)
        vb = v_ref[rows, :].astype(BF16)

        scores = lax.dot_general(qb, kb, (((1,), (1,)), ((), ())), preferred_element_type=F32)
        inner = jnp.dot((scores * dmask_ref[...]).astype(BF16), vb, preferred_element_type=F32)
        state = state_ref[...]
        cross = jnp.dot(qb, state.astype(BF16), preferred_element_type=F32)
        o = inner + cross * xi

        kz = (kr * zeta).astype(BF16)
        upd = lax.dot_general(kz, vb, (((0,), (0,)), ((), ())), preferred_element_type=F32)
        state_ref[...] = state * gamma_blk + upd

        mu = jnp.mean(o, axis=-1, keepdims=True)
        dev = o - mu
        var = jnp.mean(dev * dev, axis=-1, keepdims=True)
        normed = dev * lax.rsqrt(var + GN_EPS) * gnw
        o_ref[rows, :] = (_silu(g_ref[rows, :].astype(F32)) * normed).astype(o_ref.dtype)


def retention_core(proj, gn_w, *, batch, seq, name):
    t = proj.shape[0]
    n_heads = proj.shape[1] // (2 * RET_HEAD_DK + 2 * RET_HEAD_DV)
    step, blk = RET_STEP, RET_BLK
    steps = seq // step
    dv_per_dk = RET_HEAD_DV // RET_HEAD_DK
    k_off = n_heads
    v_off = 2 * n_heads // dv_per_dk
    g_off = v_off + n_heads

    half = RET_HEAD_DK // 2
    inv_freq = ROPE_BASE ** (-jnp.arange(half, dtype=F32) / half)
    ang = jnp.arange(seq).astype(F32)[:, None] * inv_freq[None, :]
    cos, sin = jnp.cos(ang), jnp.sin(ang)
    log_gamma = jnp.log1p(-jnp.exp2(-5.0 - jnp.arange(n_heads, dtype=F32)))
    tab = jnp.concatenate([log_gamma, jnp.exp(blk * log_gamma)])

    row = lambda b, h, c: b * steps + c
    return pl.pallas_call(
        functools.partial(_retention_kernel, n_heads=n_heads, blk=blk),
        grid=(batch, n_heads, steps),
        in_specs=[pl.BlockSpec(memory_space=pltpu.SMEM),
                  pl.BlockSpec((step, RET_HEAD_DK), lambda b, h, c: (row(b, h, c), h)),
                  pl.BlockSpec((step, RET_HEAD_DK), lambda b, h, c: (row(b, h, c), k_off + h)),
                  pl.BlockSpec((step, RET_HEAD_DV), lambda b, h, c: (row(b, h, c), v_off + h)),
                  pl.BlockSpec((step, RET_HEAD_DV), lambda b, h, c: (row(b, h, c), g_off + h)),
                  pl.BlockSpec((step, half), lambda b, h, c: (c, 0)),
                  pl.BlockSpec((step, half), lambda b, h, c: (c, 0)),
                  pl.BlockSpec((1, RET_HEAD_DV), lambda b, h, c: (0, h))],
        out_specs=pl.BlockSpec((step, RET_HEAD_DV), lambda b, h, c: (row(b, h, c), h)),
        out_shape=jax.ShapeDtypeStruct((t, n_heads * RET_HEAD_DV), BF16),
        scratch_shapes=[pltpu.VMEM((RET_HEAD_DK, RET_HEAD_DV), F32),
                        pltpu.VMEM((blk, blk), F32),
                        pltpu.VMEM((blk, V7X_LANES), F32),
                        pltpu.VMEM((blk, V7X_LANES), F32)],
        compiler_params=_params(("parallel", "parallel", "arbitrary")),
        name=name,
    )(tab, proj, proj, proj, proj, cos, sin, gn_w.reshape(1, -1))


def _softplus(x):
    return jnp.maximum(x, 0.0) + jnp.log1p(jnp.exp(-jnp.abs(x)))


def _split_bf16(v, terms):
    parts = []
    for _ in range(terms - 1):
        part = v.astype(BF16)
        parts.append(part)
        v = v - part.astype(F32)
    parts.append(v.astype(BF16))
    return jnp.concatenate(parts, axis=0)


def _sum_terms(stacked, rows):
    out = stacked[0:rows]
    for i in range(1, stacked.shape[0] // rows):
        out = out + stacked[i * rows:(i + 1) * rows]
    return out


def _ssd_decay_kernel(dt_ref, prm_ref, src_ref, acum_ref, inw_ref, *, blk):
    prm = prm_ref[...]
    a2 = -jnp.exp(prm[:, 1:2]) * LOG2_E
    ri = lax.broadcasted_iota(jnp.int32, (blk, blk), 0)
    ci = lax.broadcasted_iota(jnp.int32, (blk, blk), 1)
    upper_ones = jnp.where(ri <= ci, 1.0, 0.0).astype(BF16)
    n_heads = dt_ref.shape[0]
    for sub in range(dt_ref.shape[1] // blk):
        cols = pl.ds(sub * blk, blk)
        dt = _softplus(dt_ref[:, cols] + prm[:, 0:1])
        acum = _sum_terms(jnp.dot(_split_bf16(dt * a2, 3), upper_ones,
                                  preferred_element_type=F32), n_heads)
        src_ref[:, cols] = acum - jnp.log2(dt)
        acum_ref[:, cols] = acum
        inw_ref[:, cols] = dt * jnp.exp2(acum[:, blk - 1:blk] - acum)


def ssd_decays(dt_raw_t, dt_bias, a_log, *, blk, name):
    n_heads, t = dt_raw_t.shape
    step = SSD_DECAY_STEP
    spec = pl.BlockSpec((n_heads, step), lambda i: (0, i))
    shape = jax.ShapeDtypeStruct((n_heads, t), F32)
    return pl.pallas_call(
        functools.partial(_ssd_decay_kernel, blk=blk),
        grid=(t // step,),
        in_specs=[spec, pl.BlockSpec((n_heads, 2), lambda i: (0, 0))],
        out_specs=[spec, spec, spec],
        out_shape=[shape, shape, shape],
        compiler_params=_params(("parallel",)),
        name=name,
    )(dt_raw_t, jnp.stack([dt_bias, a_log], axis=-1))


def _ssd_kernel(z_ref, x_ref, b_ref, c_ref, wx_ref, wb_ref, wc_ref, bx_ref, bb_ref, bc_ref,
                src_ref, acum_ref, inw_ref, dexp_ref, nw_ref, o_ref,
                state_ref, xe_ref, be_ref, ce_ref, *, blk):
    step_rows = x_ref.shape[0]
    halo = V7X_SUBLANES

    @pl.when(pl.program_id(2) == 0)
    def _():
        state_ref[...] = jnp.zeros_like(state_ref)
        xe_ref[0:halo, :] = jnp.zeros((halo, xe_ref.shape[1]), F32)
        be_ref[0:halo, :] = jnp.zeros((halo, be_ref.shape[1]), F32)
        ce_ref[0:halo, :] = jnp.zeros((halo, ce_ref.shape[1]), F32)

    xe_ref[halo:, :] = x_ref[...].astype(F32)
    be_ref[halo:, :] = b_ref[...].astype(F32)
    ce_ref[halo:, :] = c_ref[...].astype(F32)

    def conv_silu(ext_ref, w_ref, bias_ref, off):
        out = bias_ref[...] + ext_ref[pl.ds(halo + off, blk), :] * w_ref[SSD_CONV_W - 1:SSD_CONV_W, :]
        for s in range(1, SSD_CONV_W):
            out = out + (ext_ref[pl.ds(halo + off - s, blk), :]
                         * w_ref[SSD_CONV_W - 1 - s:SSD_CONV_W - s, :])
        return _silu(out)

    ri = lax.broadcasted_iota(jnp.int32, (blk, blk), 0)
    ci = lax.broadcasted_iota(jnp.int32, (blk, blk), 1)
    causal = ri >= ci
    lane = lax.broadcasted_iota(jnp.int32, (blk, V7X_LANES), 1)
    first = lane < SSD_HEADDIM
    keep_first = jnp.where(first, 1.0, 0.0).astype(BF16)
    keep_second = jnp.where(first, 0.0, 1.0).astype(BF16)
    dexp = dexp_ref[...]
    nw = nw_ref[...]
    n_heads = acum_ref.shape[0]
    gw = x_ref.shape[1]

    def head_selector(terms, lanes_per_head):
        k = lax.broadcasted_iota(jnp.int32, (terms * n_heads, n_heads * lanes_per_head), 0)
        n = lax.broadcasted_iota(jnp.int32, (terms * n_heads, n_heads * lanes_per_head), 1)
        shift = int(math.log2(lanes_per_head))
        return jnp.where((k & (n_heads - 1)) == jnp.right_shift(n, shift), 1.0, 0.0).astype(BF16)

    sel_mask = head_selector(3, blk)
    sel_chan = head_selector(2, SSD_HEADDIM)
    transposed_lhs = (((0,), (0,)), ((), ()))

    for sub in range(step_rows // blk):
        off = sub * blk
        src_r = src_ref[:, pl.ds(off, blk)]
        acum_r = acum_ref[:, pl.ds(off, blk)]
        acum_bc = lax.dot_general(_split_bf16(acum_r, 3), sel_mask, transposed_lhs,
                                  preferred_element_type=F32)
        in_weight = lax.dot_general(_split_bf16(inw_ref[:, pl.ds(off, blk)], 2), sel_chan,
                                    transposed_lhs, preferred_element_type=F32)
        xs = conv_silu(xe_ref, wx_ref, bx_ref, off)
        bmb = conv_silu(be_ref, wb_ref, bb_ref, off).astype(BF16)
        cmb = conv_silu(ce_ref, wc_ref, bc_ref, off).astype(BF16)

        cb = lax.dot_general(cmb, bmb, (((1,), (1,)), ((), ())), preferred_element_type=F32)
        state = state_ref[...]
        y_off = jnp.dot(cmb, state.astype(BF16), preferred_element_type=F32)

        y_tiles, decay_tiles = [], []
        for pair in range(gw // V7X_LANES):
            r0, r1 = 2 * pair, 2 * pair + 1
            cols = slice(pair * V7X_LANES, (pair + 1) * V7X_LANES)
            xp = xs[:, cols]
            xpb = xp.astype(BF16)
            out_decay = jnp.exp2(jnp.where(first, acum_bc[:, r0 * blk:r0 * blk + V7X_LANES],
                                           acum_bc[:, r1 * blk:r1 * blk + V7X_LANES]))
            y = y_off[:, cols] * out_decay + dexp[:, cols] * xp
            for r, keep in ((r0, keep_first), (r1, keep_second)):
                seg = acum_bc[:, r * blk:(r + 1) * blk] - src_r[r:r + 1, :]
                lmat = jnp.exp2(jnp.where(causal, seg, -jnp.inf))
                y = y + jnp.dot((cb * lmat).astype(BF16), xpb * keep,
                                preferred_element_type=F32)
            y_tiles.append(y)
            decay_tiles.append(out_decay[blk - 1:blk, :])
        y = jnp.concatenate(y_tiles, axis=1)
        xw = (xs * in_weight).astype(BF16)
        upd = lax.dot_general(bmb, xw, transposed_lhs, preferred_element_type=F32)
        state_ref[...] = state * jnp.concatenate(decay_tiles, axis=1) + upd

        yg = y * _silu(z_ref[pl.ds(off, blk), :].astype(F32))
        o_ref[pl.ds(off, blk), :] = (yg * _rms_scale(yg) * nw).astype(o_ref.dtype)

    xe_ref[0:halo, :] = xe_ref[step_rows:step_rows + halo, :]
    be_ref[0:halo, :] = be_ref[step_rows:step_rows + halo, :]
    ce_ref[0:halo, :] = ce_ref[step_rows:step_rows + halo, :]


def ssd_core(zxbc, dt_raw, conv_w, conv_b, dt_bias, a_log, d_skip, norm_w, *, batch, seq, name):
    t = zxbc.shape[0]
    g, r, n, gw = SSD_GROUPS, SSD_HEADS_PER_GROUP, SSD_STATE, SSD_GROUP_WIDTH
    d_inner = g * gw
    step, blk = SSD_STEP, SSD_BLK
    steps = seq // step
    x_off = d_inner // gw
    b_off = 2 * d_inner // n
    c_off = b_off + g
    wb_off = d_inner // n
    wc_off = wb_off + g

    src_r, acum_r, inw_r = (a.reshape(g, r, t) for a in
                           ssd_decays(dt_raw.T, dt_bias, a_log, blk=blk, name=name + "_decays"))
    dexp = jnp.repeat(d_skip, SSD_HEADDIM).reshape(1, d_inner)
    conv_b = conv_b.reshape(1, -1)

    row = lambda b, gi, c: b * steps + c
    return pl.pallas_call(
        functools.partial(_ssd_kernel, blk=blk),
        grid=(batch, g, steps),
        in_specs=[pl.BlockSpec((step, gw), lambda b, gi, c: (row(b, gi, c), gi)),
                  pl.BlockSpec((step, gw), lambda b, gi, c: (row(b, gi, c), x_off + gi)),
                  pl.BlockSpec((step, n), lambda b, gi, c: (row(b, gi, c), b_off + gi)),
                  pl.BlockSpec((step, n), lambda b, gi, c: (row(b, gi, c), c_off + gi)),
                  pl.BlockSpec((SSD_CONV_W, gw), lambda b, gi, c: (0, gi)),
                  pl.BlockSpec((SSD_CONV_W, n), lambda b, gi, c: (0, wb_off + gi)),
                  pl.BlockSpec((SSD_CONV_W, n), lambda b, gi, c: (0, wc_off + gi)),
                  pl.BlockSpec((1, gw), lambda b, gi, c: (0, gi)),
                  pl.BlockSpec((1, n), lambda b, gi, c: (0, wb_off + gi)),
                  pl.BlockSpec((1, n), lambda b, gi, c: (0, wc_off + gi)),
                  pl.BlockSpec((None, r, step), lambda b, gi, c: (gi, 0, row(b, gi, c))),
                  pl.BlockSpec((None, r, step), lambda b, gi, c: (gi, 0, row(b, gi, c))),
                  pl.BlockSpec((None, r, step), lambda b, gi, c: (gi, 0, row(b, gi, c))),
                  pl.BlockSpec((1, gw), lambda b, gi, c: (0, gi)),
                  pl.BlockSpec((1, gw), lambda b, gi, c: (0, gi))],
        out_specs=pl.BlockSpec((step, gw), lambda b, gi, c: (row(b, gi, c), gi)),
        out_shape=jax.ShapeDtypeStruct((t, d_inner), BF16),
        scratch_shapes=[pltpu.VMEM((n, gw), F32),
                        pltpu.VMEM((step + V7X_SUBLANES, gw), F32),
                        pltpu.VMEM((step + V7X_SUBLANES, n), F32),
                        pltpu.VMEM((step + V7X_SUBLANES, n), F32)],
        compiler_params=_params(("parallel", "parallel", "arbitrary")),
        name=name,
    )(zxbc, zxbc, zxbc, zxbc, conv_w, conv_w, conv_w, conv_b, conv_b, conv_b,
      src_r, acum_r, inw_r, dexp, norm_w.reshape(1, d_inner))


def kernel(x, norm_mix_pre, norm_mix_post, norm_ffn_pre, norm_ffn_post, ret_w_in, ret_gn_w, ret_w_out, ssd_w_in, ssd_conv_w, ssd_conv_b, ssd_dt_bias, ssd_a_log, ssd_d, ssd_norm_w, ssd_w_out, mlp_w_up, mlp_w_down):
    batch, seq, d = x.shape
    h = x.reshape(batch * seq, d)
    act_dtype = F32

    n_zxbc = ssd_w_in.shape[2] - SSD_GROUPS * SSD_HEADS_PER_GROUP
    w_dt = ssd_w_in[0, :, n_zxbc:].astype(BF16)
    ret_w_in, ret_w_out, ssd_w_in, ssd_w_out, mlp_w_up, mlp_w_down = (
        w.astype(BF16) for w in (ret_w_in, ret_w_out, ssd_w_in, ssd_w_out, mlp_w_up, mlp_w_down))

    proj = norm_matmul(h, norm_mix_pre, 0, ret_w_in, 0, n_out=ret_w_in.shape[2],
                       out_dtype=act_dtype, name="ret_in_proj")
    y = retention_core(proj, ret_gn_w[0], batch=batch, seq=seq, name="retention_core")
    h = out_proj_residual(y, ret_w_out, 0, h, norm_mix_post, 0, name="ret_out_proj")
    h = mlp_residual(h, norm_ffn_pre, mlp_w_up, mlp_w_down, norm_ffn_post, 0, name="mlp0")

    zxbc, dt_raw = norm_matmul(h, norm_mix_pre, 1, ssd_w_in, 0, n_out=n_zxbc,
                               w_extra=w_dt, out_dtype=act_dtype, name="ssd_in_proj")
    y = ssd_core(zxbc, dt_raw, ssd_conv_w[0], ssd_conv_b[0], ssd_dt_bias[0], ssd_a_log[0],
                 ssd_d[0], ssd_norm_w[0], batch=batch, seq=seq, name="ssd_core")
    h = out_proj_residual(y, ssd_w_out, 0, h, norm_mix_post, 1, name="ssd_out_proj")
    h = mlp_residual(h, norm_ffn_pre, mlp_w_up, mlp_w_down, norm_ffn_post, 1, name="mlp1")
    return h.reshape(batch, seq, d)
```

```python
import functools
import math

import jax
import jax.numpy as jnp
from jax import lax
from jax.experimental import pallas as pl
from jax.experimental.pallas import tpu as pltpu

F32 = jnp.float32
BF16 = jnp.bfloat16

RMS_EPS = 1e-6
GN_EPS = 1e-5
ROPE_BASE = 10000.0
CHUNK = 64
LOG2_E = math.log2(math.e)

RET_HEAD_DK = 256
RET_HEAD_DV = 512
SSD_HEADDIM = 64
SSD_GROUPS = 8
SSD_HEADS_PER_GROUP = 8
SSD_STATE = 128
SSD_CONV_W = 4
SSD_GROUP_WIDTH = SSD_HEADS_PER_GROUP * SSD_HEADDIM

V7X_LANES = 128
V7X_SUBLANES = 8
V7X_VMEM_LIMIT_BYTES = 56 * 1024 * 1024

PROJ_TM, PROJ_TN = 1024, 1024
OUT_TM, OUT_TK = 512, 2048
MLP_TM, MLP_TF = 512, 1024
RET_STEP, RET_BLK = 1024, 256
SSD_DECAY_STEP = 2048
SSD_STEP, SSD_BLK = 1024, 128


def _params(semantics):
    return pltpu.CompilerParams(dimension_semantics=semantics,
                                vmem_limit_bytes=V7X_VMEM_LIMIT_BYTES)


def _rms_scale(x):
    return lax.rsqrt(jnp.mean(x * x, axis=-1, keepdims=True) + RMS_EPS)


def _silu(x):
    half = 0.5 * x
    return half + half * jnp.tanh(half)


def _norm_matmul_kernel(x_ref, g_ref, w_ref, o_ref, u_ref):
    @pl.when(pl.program_id(1) == 0)
    def _():
        x = x_ref[...]
        u_ref[...] = (x * _rms_scale(x) * g_ref[...]).astype(BF16)

    o_ref[...] = jnp.dot(u_ref[...], w_ref[...],
                         preferred_element_type=F32).astype(o_ref.dtype)


def _matmul_dt_kernel(u_ref, w_ref, wdt_ref, o_ref, dt_ref):
    @pl.when(pl.program_id(1) == 0)
    def _():
        dt_ref[...] = lax.dot_general(wdt_ref[...], u_ref[...], (((1,), (1,)), ((), ())),
                                      preferred_element_type=F32)

    o_ref[...] = jnp.dot(u_ref[...], w_ref[...],
                         preferred_element_type=F32).astype(o_ref.dtype)


def _gain_spec(d, layer):
    return pl.BlockSpec((None, 1, d), lambda *_: (layer, 0, 0))


def _gains(g):
    return g.reshape(g.shape[0], 1, g.shape[1])


def norm_matmul(h, g, layer, w, w_layer, *, out_dtype, name):
    t, d = h.shape
    n = w.shape[2]
    tm, tn = PROJ_TM, PROJ_TN
    return pl.pallas_call(
        _norm_matmul_kernel,
        grid=(t // tm, n // tn),
        in_specs=[pl.BlockSpec((tm, d), lambda i, j: (i, 0)),
                  _gain_spec(d, layer),
                  pl.BlockSpec((None, d, tn), lambda i, j: (w_layer, 0, j))],
        out_specs=pl.BlockSpec((tm, tn), lambda i, j: (i, j)),
        out_shape=jax.ShapeDtypeStruct((t, n), out_dtype),
        scratch_shapes=[pltpu.VMEM((tm, d), BF16)],
        compiler_params=_params(("parallel", "arbitrary")),
        name=name,
    )(h, _gains(g), w)


def matmul_dt(u, w, w_layer, w_dt_t, *, n_out, out_dtype, name):
    t, d = u.shape
    n_dt = w_dt_t.shape[0]
    tm, tn = PROJ_TM, PROJ_TN
    return pl.pallas_call(
        _matmul_dt_kernel,
        grid=(t // tm, n_out // tn),
        in_specs=[pl.BlockSpec((tm, d), lambda i, j: (i, 0)),
                  pl.BlockSpec((None, d, tn), lambda i, j: (w_layer, 0, j)),
                  pl.BlockSpec((n_dt, d), lambda i, j: (0, 0))],
        out_specs=[pl.BlockSpec((tm, tn), lambda i, j: (i, j)),
                   pl.BlockSpec((n_dt, tm), lambda i, j: (0, i))],
        out_shape=[jax.ShapeDtypeStruct((t, n_out), out_dtype),
                   jax.ShapeDtypeStruct((n_dt, t), F32)],
        compiler_params=_params(("parallel", "arbitrary")),
        name=name,
    )(u, w, w_dt_t)


def _residual_norm(h, m, g_post, g_next=None):
    hn = h + m * _rms_scale(m) * g_post
    if g_next is None:
        return hn, None
    return hn, (hn * _rms_scale(hn) * g_next).astype(BF16)


def _lagged_residual_steps(accumulate, acc_ref, h_ref, gpost_ref, gnext_ref, o_ref, un_ref,
                           n_slices):
    i = pl.program_id(0)
    j = pl.program_id(1)
    last = pl.num_programs(0) - 1
    slice_rows = o_ref.shape[0] // n_slices

    def add(part):
        slot = i % 2
        acc_ref[slot] = jnp.where(j == 0, part, acc_ref[slot] + part)

    def finish_slice():
        rows = pl.ds(pl.multiple_of(j * slice_rows, slice_rows), slice_rows)
        prev = (i + 1) % 2
        m = acc_ref[prev, rows, :]
        hn, un = _residual_norm(h_ref[rows, :], m, gpost_ref[...],
                                None if gnext_ref is None else gnext_ref[...])
        o_ref[rows, :] = hn
        if un_ref is not None:
            un_ref[rows, :] = un

    @pl.when((i == 0) & (j == 0))
    def _():
        acc_ref[...] = jnp.zeros_like(acc_ref)

    @pl.when(i == 0)
    def _():
        accumulate(add)

    @pl.when((i > 0) & (i < last))
    def _():
        finish_slice()
        accumulate(add)

    @pl.when(i == last)
    def _():
        finish_slice()


def _lagged_tile_maps(n_tiles, n_steps):
    tile_in = lambda i, j: jnp.minimum(i, n_tiles - 1)
    tile_out = lambda i, j: jnp.maximum(i - 1, 0)
    step = lambda i, j: jnp.where(i < n_tiles, j, n_steps - 1)
    return tile_in, tile_out, step


def _out_proj_kernel(y_ref, w_ref, h_ref, gpost_ref, gnext_ref, o_ref, un_ref, acc_ref, *, n_k):
    def accumulate(add):
        add(jnp.dot(y_ref[...], w_ref[...], preferred_element_type=F32))

    _lagged_residual_steps(accumulate, acc_ref, h_ref, gpost_ref, gnext_ref, o_ref, un_ref, n_k)


def out_proj_residual(y, w, w_layer, h, g_post, layer, g_next, next_layer, *, name):
    t, k = y.shape
    d = w.shape[2]
    tm, tk = OUT_TM, OUT_TK
    n, n_k = t // tm, k // tk
    tile_in, tile_out, step = _lagged_tile_maps(n, n_k)
    finished = pl.BlockSpec((tm, d), lambda i, j: (tile_out(i, j), 0))
    return pl.pallas_call(
        functools.partial(_out_proj_kernel, n_k=n_k),
        grid=(n + 1, n_k),
        in_specs=[pl.BlockSpec((tm, tk), lambda i, j: (tile_in(i, j), step(i, j))),
                  pl.BlockSpec((None, tk, d), lambda i, j: (w_layer, step(i, j), 0)),
                  finished,
                  _gain_spec(d, layer),
                  _gain_spec(d, next_layer)],
        out_specs=[finished, finished],
        out_shape=[jax.ShapeDtypeStruct((t, d), F32), jax.ShapeDtypeStruct((t, d), BF16)],
        scratch_shapes=[pltpu.VMEM((2, tm, d), F32)],
        compiler_params=_params(("arbitrary", "arbitrary")),
        name=name,
    )(y, w, h, _gains(g_post), _gains(g_next))


def _mlp_kernel(u_ref, h_ref, wup_ref, wdn_ref, gpost_ref, *rest, n_f, emit_u):
    if emit_u:
        gnext_ref, o_ref, un_ref, acc_ref = rest
    else:
        (o_ref, acc_ref), gnext_ref, un_ref = rest, None, None

    def accumulate(add):
        hid = jnp.maximum(jnp.dot(u_ref[...], wup_ref[...], preferred_element_type=F32), 0.0)
        add(jnp.dot((hid * hid).astype(BF16), wdn_ref[...], preferred_element_type=F32))

    _lagged_residual_steps(accumulate, acc_ref, h_ref, gpost_ref, gnext_ref, o_ref, un_ref, n_f)


def mlp_residual(u, h, wup, wdn, gpost, layer, g_next=None, next_layer=None, *, name):
    t, d = h.shape
    f = wup.shape[2]
    tm, tf = MLP_TM, MLP_TF
    n, n_f = t // tm, f // tf
    emit_u = g_next is not None
    tile_in, tile_out, step = _lagged_tile_maps(n, n_f)
    finished = pl.BlockSpec((tm, d), lambda i, j: (tile_out(i, j), 0))
    in_specs = [pl.BlockSpec((tm, d), lambda i, j: (tile_in(i, j), 0)),
                finished,
                pl.BlockSpec((None, d, tf), lambda i, j: (layer, 0, step(i, j))),
                pl.BlockSpec((None, tf, d), lambda i, j: (layer, step(i, j), 0)),
                _gain_spec(d, layer)]
    args = [u, h, wup, wdn, _gains(gpost)]
    out_specs = [finished]
    out_shape = [jax.ShapeDtypeStruct((t, d), F32)]
    if emit_u:
        in_specs.append(_gain_spec(d, next_layer))
        args.append(_gains(g_next))
        out_specs.append(finished)
        out_shape.append(jax.ShapeDtypeStruct((t, d), BF16))
    return pl.pallas_call(
        functools.partial(_mlp_kernel, n_f=n_f, emit_u=emit_u),
        grid=(n + 1, n_f),
        in_specs=in_specs,
        out_specs=out_specs,
        out_shape=out_shape,
        scratch_shapes=[pltpu.VMEM((2, tm, d), F32)],
        compiler_params=_params(("arbitrary", "arbitrary")),
        name=name,
    )(*args)


def _retention_kernel(tab_ref, q_ref, k_ref, v_ref, g_ref, cos_ref, sin_ref, gnw_ref,
                      o_ref, state_ref, dmask_ref, xi_ref, zeta_ref, *, n_heads, blk):
    head = pl.program_id(1)
    half = RET_HEAD_DK // 2
    log_gamma = tab_ref[head]
    gamma_blk = tab_ref[n_heads + head]

    @pl.when(pl.program_id(2) == 0)
    def _():
        state_ref[...] = jnp.zeros_like(state_ref)
        ri = lax.broadcasted_iota(jnp.int32, (blk, blk), 0)
        ci = lax.broadcasted_iota(jnp.int32, (blk, blk), 1)
        dist = jnp.abs(ri - ci).astype(F32)
        shift = int(math.log2(CHUNK))
        visible = jnp.right_shift(ci, shift) <= jnp.right_shift(ri, shift)
        dmask_ref[...] = jnp.where(visible, jnp.exp(dist * log_gamma), 0.0)
        pos = lax.broadcasted_iota(jnp.int32, (blk, V7X_LANES), 0).astype(F32)
        xi_ref[...] = jnp.exp((pos + 1.0) * log_gamma)
        zeta_ref[...] = jnp.exp((blk - 1.0 - pos) * log_gamma)

    xi = jnp.concatenate([xi_ref[...]] * (RET_HEAD_DV // V7X_LANES), axis=1)
    zeta = jnp.concatenate([zeta_ref[...]] * (RET_HEAD_DK // V7X_LANES), axis=1)
    gnw = gnw_ref[...]

    for sub in range(q_ref.shape[0] // blk):
        rows = pl.ds(sub * blk, blk)
        cos = cos_ref[rows, :]
        sin = sin_ref[rows, :]

        def rotary(t):
            t1, t2 = t[:, :half], t[:, half:]
            return jnp.concatenate([t1 * cos - t2 * sin, t1 * sin + t2 * cos], axis=1)

        qr = rotary(q_ref[rows, :].astype(F32))
        kr = rotary(k_ref[rows, :].astype(F32)) * (RET_HEAD_DK ** -0.5)
        qb = qr.astype(BF16)
        kb = kr.astype(BF16)
        vb = v_ref[rows, :].astype(BF16)

        scores = lax.dot_general(qb, kb, (((1,), (1,)), ((), ())), preferred_element_type=F32)
        inner = jnp.dot((scores * dmask_ref[...]).astype(BF16), vb, preferred_element_type=F32)
        state = state_ref[...]
        cross = jnp.dot(qb, state.astype(BF16), preferred_element_type=F32)
        o = inner + cross * xi

        kz = (kr * zeta).astype(BF16)
        upd = lax.dot_general(kz, vb, (((0,), (0,)), ((), ())), preferred_element_type=F32)
        state_ref[...] = state * gamma_blk + upd

        mu = jnp.mean(o, axis=-1, keepdims=True)
        dev = o - mu
        var = jnp.mean(dev * dev, axis=-1, keepdims=True)
        normed = dev * lax.rsqrt(var + GN_EPS) * gnw
        o_ref[rows, :] = (_silu(g_ref[rows, :].astype(F32)) * normed).astype(o_ref.dtype)


def retention_core(proj, gn_w, *, batch, seq, name):
    t = proj.shape[0]
    n_heads = proj.shape[1] // (2 * RET_HEAD_DK + 2 * RET_HEAD_DV)
    step, blk = RET_STEP, RET_BLK
    steps = seq // step
    dv_per_dk = RET_HEAD_DV // RET_HEAD_DK
    k_off = n_heads
    v_off = 2 * n_heads // dv_per_dk
    g_off = v_off + n_heads

    half = RET_HEAD_DK // 2
    inv_freq = ROPE_BASE ** (-jnp.arange(half, dtype=F32) / half)
    ang = jnp.arange(seq).astype(F32)[:, None] * inv_freq[None, :]
    cos, sin = jnp.cos(ang), jnp.sin(ang)
    log_gamma = jnp.log1p(-jnp.exp2(-5.0 - jnp.arange(n_heads, dtype=F32)))
    tab = jnp.concatenate([log_gamma, jnp.exp(blk * log_gamma)])

    row = lambda b, h, c: b * steps + c
    return pl.pallas_call(
        functools.partial(_retention_kernel, n_heads=n_heads, blk=blk),
        grid=(batch, n_heads, steps),
        in_specs=[pl.BlockSpec(memory_space=pltpu.SMEM),
                  pl.BlockSpec((step, RET_HEAD_DK), lambda b, h, c: (row(b, h, c), h)),
                  pl.BlockSpec((step, RET_HEAD_DK), lambda b, h, c: (row(b, h, c), k_off + h)),
                  pl.BlockSpec((step, RET_HEAD_DV), lambda b, h, c: (row(b, h, c), v_off + h)),
                  pl.BlockSpec((step, RET_HEAD_DV), lambda b, h, c: (row(b, h, c), g_off + h)),
                  pl.BlockSpec((step, half), lambda b, h, c: (c, 0)),
                  pl.BlockSpec((step, half), lambda b, h, c: (c, 0)),
                  pl.BlockSpec((1, RET_HEAD_DV), lambda b, h, c: (0, h))],
        out_specs=pl.BlockSpec((step, RET_HEAD_DV), lambda b, h, c: (row(b, h, c), h)),
        out_shape=jax.ShapeDtypeStruct((t, n_heads * RET_HEAD_DV), BF16),
        scratch_shapes=[pltpu.VMEM((RET_HEAD_DK, RET_HEAD_DV), F32),
                        pltpu.VMEM((blk, blk), F32),
                        pltpu.VMEM((blk, V7X_LANES), F32),
                        pltpu.VMEM((blk, V7X_LANES), F32)],
        compiler_params=_params(("parallel", "parallel", "arbitrary")),
        name=name,
    )(tab, proj, proj, proj, proj, cos, sin, gn_w.reshape(1, -1))


def _softplus(x):
    return jnp.maximum(x, 0.0) + jnp.log1p(jnp.exp(-jnp.abs(x)))


def _split_bf16(v, terms):
    parts = []
    for _ in range(terms - 1):
        part = v.astype(BF16)
        parts.append(part)
        v = v - part.astype(F32)
    parts.append(v.astype(BF16))
    return jnp.concatenate(parts, axis=0)


def _sum_terms(stacked, rows):
    out = stacked[0:rows]
    for i in range(1, stacked.shape[0] // rows):
        out = out + stacked[i * rows:(i + 1) * rows]
    return out


def _ssd_decay_kernel(dt_ref, prm_ref, src_ref, acum_ref, inw_ref, *, blk):
    prm = prm_ref[...]
    a2 = -jnp.exp(prm[:, 1:2]) * LOG2_E
    ri = lax.broadcasted_iota(jnp.int32, (blk, blk), 0)
    ci = lax.broadcasted_iota(jnp.int32, (blk, blk), 1)
    upper_ones = jnp.where(ri <= ci, 1.0, 0.0).astype(BF16)
    n_heads = dt_ref.shape[0]
    for sub in range(dt_ref.shape[1] // blk):
        cols = pl.ds(sub * blk, blk)
        dt = _softplus(dt_ref[:, cols] + prm[:, 0:1])
        acum = _sum_terms(jnp.dot(_split_bf16(dt * a2, 3), upper_ones,
                                  preferred_element_type=F32), n_heads)
        src_ref[:, cols] = acum - jnp.log2(dt)
        acum_ref[:, cols] = acum
        inw_ref[:, cols] = dt * jnp.exp2(acum[:, blk - 1:blk] - acum)


def ssd_decays(dt_raw_t, dt_bias, a_log, *, blk, name):
    n_heads, t = dt_raw_t.shape
    step = SSD_DECAY_STEP
    spec = pl.BlockSpec((n_heads, step), lambda i: (0, i))
    shape = jax.ShapeDtypeStruct((n_heads, t), F32)
    return pl.pallas_call(
        functools.partial(_ssd_decay_kernel, blk=blk),
        grid=(t // step,),
        in_specs=[spec, pl.BlockSpec((n_heads, 2), lambda i: (0, 0))],
        out_specs=[spec, spec, spec],
        out_shape=[shape, shape, shape],
        compiler_params=_params(("parallel",)),
        name=name,
    )(dt_raw_t, jnp.stack([dt_bias, a_log], axis=-1))


def _ssd_kernel(z_ref, x_ref, b_ref, c_ref, wx_ref, wb_ref, wc_ref, bx_ref, bb_ref, bc_ref,
                src_ref, acum_ref, inw_ref, dexp_ref, nw_ref, o_ref,
                state_ref, xe_ref, be_ref, ce_ref, *, blk):
    step_rows = x_ref.shape[0]
    halo = V7X_SUBLANES

    @pl.when(pl.program_id(2) == 0)
    def _():
        state_ref[...] = jnp.zeros_like(state_ref)
        xe_ref[0:halo, :] = jnp.zeros((halo, xe_ref.shape[1]), F32)
        be_ref[0:halo, :] = jnp.zeros((halo, be_ref.shape[1]), F32)
        ce_ref[0:halo, :] = jnp.zeros((halo, ce_ref.shape[1]), F32)

    xe_ref[halo:, :] = x_ref[...].astype(F32)
    be_ref[halo:, :] = b_ref[...].astype(F32)
    ce_ref[halo:, :] = c_ref[...].astype(F32)

    def conv_silu(ext_ref, w_ref, bias_ref, off):
        out = bias_ref[...] + ext_ref[pl.ds(halo + off, blk), :] * w_ref[SSD_CONV_W - 1:SSD_CONV_W, :]
        for s in range(1, SSD_CONV_W):
            out = out + (ext_ref[pl.ds(halo + off - s, blk), :]
                         * w_ref[SSD_CONV_W - 1 - s:SSD_CONV_W - s, :])
        return _silu(out)

    ri = lax.broadcasted_iota(jnp.int32, (blk, blk), 0)
    ci = lax.broadcasted_iota(jnp.int32, (blk, blk), 1)
    causal = ri >= ci
    lane = lax.broadcasted_iota(jnp.int32, (blk, V7X_LANES), 1)
    first = lane < SSD_HEADDIM
    keep_first = jnp.where(first, 1.0, 0.0).astype(BF16)
    keep_second = jnp.where(first, 0.0, 1.0).astype(BF16)
    dexp = dexp_ref[...]
    nw = nw_ref[...]
    n_heads = acum_ref.shape[0]
    gw = x_ref.shape[1]

    def head_selector(terms, lanes_per_head):
        k = lax.broadcasted_iota(jnp.int32, (terms * n_heads, n_heads * lanes_per_head), 0)
        n = lax.broadcasted_iota(jnp.int32, (terms * n_heads, n_heads * lanes_per_head), 1)
        shift = int(math.log2(lanes_per_head))
        return jnp.where((k & (n_heads - 1)) == jnp.right_shift(n, shift), 1.0, 0.0).astype(BF16)

    sel_mask = head_selector(3, blk)
    sel_chan = head_selector(2, SSD_HEADDIM)
    transposed_lhs = (((0,), (0,)), ((), ()))

    for sub in range(step_rows // blk):
        off = sub * blk
        src_r = src_ref[:, pl.ds(off, blk)]
        acum_r = acum_ref[:, pl.ds(off, blk)]
        acum_bc = lax.dot_general(_split_bf16(acum_r, 3), sel_mask, transposed_lhs,
                                  preferred_element_type=F32)
        in_weight = lax.dot_general(_split_bf16(inw_ref[:, pl.ds(off, blk)], 2), sel_chan,
                                    transposed_lhs, preferred_element_type=F32)
        xs = conv_silu(xe_ref, wx_ref, bx_ref, off)
        bmb = conv_silu(be_ref, wb_ref, bb_ref, off).astype(BF16)
        cmb = conv_silu(ce_ref, wc_ref, bc_ref, off).astype(BF16)

        cb = lax.dot_general(cmb, bmb, (((1,), (1,)), ((), ())), preferred_element_type=F32)
        state = state_ref[...]
        y_off = jnp.dot(cmb, state.astype(BF16), preferred_element_type=F32)

        y_tiles, decay_tiles = [], []
        for pair in range(gw // V7X_LANES):
            r0, r1 = 2 * pair, 2 * pair + 1
            cols = slice(pair * V7X_LANES, (pair + 1) * V7X_LANES)
            xp = xs[:, cols]
            xpb = xp.astype(BF16)
            out_decay = jnp.exp2(jnp.where(first, acum_bc[:, r0 * blk:r0 * blk + V7X_LANES],
                                           acum_bc[:, r1 * blk:r1 * blk + V7X_LANES]))
            y = y_off[:, cols] * out_decay + dexp[:, cols] * xp
            for r, keep in ((r0, keep_first), (r1, keep_second)):
                seg = acum_bc[:, r * blk:(r + 1) * blk] - src_r[r:r + 1, :]
                lmat = jnp.exp2(jnp.where(causal, seg, -jnp.inf))
                y = y + jnp.dot((cb * lmat).astype(BF16), xpb * keep,
                                preferred_element_type=F32)
            y_tiles.append(y)
            decay_tiles.append(out_decay[blk - 1:blk, :])
        y = jnp.concatenate(y_tiles, axis=1)
        xw = (xs * in_weight).astype(BF16)
        upd = lax.dot_general(bmb, xw, transposed_lhs, preferred_element_type=F32)
        state_ref[...] = state * jnp.concatenate(decay_tiles, axis=1) + upd

        yg = y * _silu(z_ref[pl.ds(off, blk), :].astype(F32))
        o_ref[pl.ds(off, blk), :] = (yg * _rms_scale(yg) * nw).astype(o_ref.dtype)

    xe_ref[0:halo, :] = xe_ref[step_rows:step_rows + halo, :]
    be_ref[0:halo, :] = be_ref[step_rows:step_rows + halo, :]
    ce_ref[0:halo, :] = ce_ref[step_rows:step_rows + halo, :]


def ssd_core(zxbc, dt_raw_t, conv_w, conv_b, dt_bias, a_log, d_skip, norm_w, *, batch, seq, name):
    t = zxbc.shape[0]
    g, r, n, gw = SSD_GROUPS, SSD_HEADS_PER_GROUP, SSD_STATE, SSD_GROUP_WIDTH
    d_inner = g * gw
    step, blk = SSD_STEP, SSD_BLK
    steps = seq // step
    x_off = d_inner // gw
    b_off = 2 * d_inner // n
    c_off = b_off + g
    wb_off = d_inner // n
    wc_off = wb_off + g

    src_r, acum_r, inw_r = (a.reshape(g, r, t) for a in
                           ssd_decays(dt_raw_t, dt_bias, a_log, blk=blk, name=name + "_decays"))
    dexp = jnp.repeat(d_skip, SSD_HEADDIM).reshape(1, d_inner)
    conv_b = conv_b.reshape(1, -1)

    row = lambda b, gi, c: b * steps + c
    return pl.pallas_call(
        functools.partial(_ssd_kernel, blk=blk),
        grid=(batch, g, steps),
        in_specs=[pl.BlockSpec((step, gw), lambda b, gi, c: (row(b, gi, c), gi)),
                  pl.BlockSpec((step, gw), lambda b, gi, c: (row(b, gi, c), x_off + gi)),
                  pl.BlockSpec((step, n), lambda b, gi, c: (row(b, gi, c), b_off + gi)),
                  pl.BlockSpec((step, n), lambda b, gi, c: (row(b, gi, c), c_off + gi)),
                  pl.BlockSpec((SSD_CONV_W, gw), lambda b, gi, c: (0, gi)),
                  pl.BlockSpec((SSD_CONV_W, n), lambda b, gi, c: (0, wb_off + gi)),
                  pl.BlockSpec((SSD_CONV_W, n), lambda b, gi, c: (0, wc_off + gi)),
                  pl.BlockSpec((1, gw), lambda b, gi, c: (0, gi)),
                  pl.BlockSpec((1, n), lambda b, gi, c: (0, wb_off + gi)),
                  pl.BlockSpec((1, n), lambda b, gi, c: (0, wc_off + gi)),
                  pl.BlockSpec((None, r, step), lambda b, gi, c: (gi, 0, row(b, gi, c))),
                  pl.BlockSpec((None, r, step), lambda b, gi, c: (gi, 0, row(b, gi, c))),
                  pl.BlockSpec((None, r, step), lambda b, gi, c: (gi, 0, row(b, gi, c))),
                  pl.BlockSpec((1, gw), lambda b, gi, c: (0, gi)),
                  pl.BlockSpec((1, gw), lambda b, gi, c: (0, gi))],
        out_specs=pl.BlockSpec((step, gw), lambda b, gi, c: (row(b, gi, c), gi)),
        out_shape=jax.ShapeDtypeStruct((t, d_inner), BF16),
        scratch_shapes=[pltpu.VMEM((n, gw), F32),
                        pltpu.VMEM((step + V7X_SUBLANES, gw), F32),
                        pltpu.VMEM((step + V7X_SUBLANES, n), F32),
                        pltpu.VMEM((step + V7X_SUBLANES, n), F32)],
        compiler_params=_params(("parallel", "parallel", "arbitrary")),
        name=name,
    )(zxbc, zxbc, zxbc, zxbc, conv_w, conv_w, conv_w, conv_b, conv_b, conv_b,
      src_r, acum_r, inw_r, dexp, norm_w.reshape(1, d_inner))


def kernel(x, norm_mix_pre, norm_mix_post, norm_ffn_pre, norm_ffn_post, ret_w_in, ret_gn_w, ret_w_out, ssd_w_in, ssd_conv_w, ssd_conv_b, ssd_dt_bias, ssd_a_log, ssd_d, ssd_norm_w, ssd_w_out, mlp_w_up, mlp_w_down):
    batch, seq, d = x.shape
    h = x.reshape(batch * seq, d)
    act_dtype = F32

    n_zxbc = ssd_w_in.shape[2] - SSD_GROUPS * SSD_HEADS_PER_GROUP
    w_dt_t = ssd_w_in[0, :, n_zxbc:].T.astype(BF16)
    ret_w_in, ret_w_out, ssd_w_in, ssd_w_out, mlp_w_up, mlp_w_down = (
        w.astype(BF16) for w in (ret_w_in, ret_w_out, ssd_w_in, ssd_w_out, mlp_w_up, mlp_w_down))

    proj = norm_matmul(h, norm_mix_pre, 0, ret_w_in, 0, out_dtype=act_dtype, name="ret_in_proj")
    y = retention_core(proj, ret_gn_w[0], batch=batch, seq=seq, name="retention_core")
    h, u = out_proj_residual(y, ret_w_out, 0, h, norm_mix_post, 0, norm_ffn_pre, 0,
                             name="ret_out_proj")
    h, u = mlp_residual(u, h, mlp_w_up, mlp_w_down, norm_ffn_post, 0, norm_mix_pre, 1, name="mlp0")

    zxbc, dt_raw_t = matmul_dt(u, ssd_w_in, 0, w_dt_t, n_out=n_zxbc, out_dtype=act_dtype,
                               name="ssd_in_proj")
    y = ssd_core(zxbc, dt_raw_t, ssd_conv_w[0], ssd_conv_b[0], ssd_dt_bias[0], ssd_a_log[0],
                 ssd_d[0], ssd_norm_w[0], batch=batch, seq=seq, name="ssd_core")
    h, u = out_proj_residual(y, ssd_w_out, 0, h, norm_mix_post, 1, norm_ffn_pre, 1,
                             name="ssd_out_proj")
    (h,) = mlp_residual(u, h, mlp_w_up, mlp_w_down, norm_ffn_post, 1, name="mlp1")
    return h.reshape(batch, seq, d)
```

```python
import functools
import math

import jax
import jax.numpy as jnp
from jax import lax
from jax.experimental import pallas as pl
from jax.experimental.pallas import tpu as pltpu

F32 = jnp.float32
BF16 = jnp.bfloat16

RMS_EPS = 1e-6
GN_EPS = 1e-5
ROPE_BASE = 10000.0
CHUNK = 64
LOG2_E = math.log2(math.e)

RET_HEAD_DK = 256
RET_HEAD_DV = 512
SSD_HEADDIM = 64
SSD_GROUPS = 8
SSD_HEADS_PER_GROUP = 8
SSD_STATE = 128
SSD_CONV_W = 4
SSD_GROUP_WIDTH = SSD_HEADS_PER_GROUP * SSD_HEADDIM

V7X_LANES = 128
V7X_SUBLANES = 8
V7X_VMEM_LIMIT_BYTES = 56 * 1024 * 1024

PROJ_TM, PROJ_TN = 1024, 1024
OUT_TM, OUT_TK = 512, 2048
MLP_TM, MLP_TF = 1024, 1024
RET_STEP, RET_BLK = 1024, 256
SSD_DECAY_STEP = 2048
SSD_STEP, SSD_BLK = 1024, 128


def _params(semantics):
    return pltpu.CompilerParams(dimension_semantics=semantics,
                                vmem_limit_bytes=V7X_VMEM_LIMIT_BYTES)


def _rms_scale(x):
    return lax.rsqrt(jnp.mean(x * x, axis=-1, keepdims=True) + RMS_EPS)


def _silu(x):
    half = 0.5 * x
    return half + half * jnp.tanh(half)


def _norm_matmul_kernel(x_ref, g_ref, w_ref, o_ref, u_ref):
    @pl.when(pl.program_id(1) == 0)
    def _():
        x = x_ref[...]
        u_ref[...] = (x * _rms_scale(x) * g_ref[...]).astype(BF16)

    o_ref[...] = jnp.dot(u_ref[...], w_ref[...],
                         preferred_element_type=F32).astype(o_ref.dtype)


def _matmul_dt_kernel(u_ref, w_ref, wdt_ref, o_ref, dt_ref):
    @pl.when(pl.program_id(1) == 0)
    def _():
        dt_ref[...] = lax.dot_general(wdt_ref[...], u_ref[...], (((1,), (1,)), ((), ())),
                                      preferred_element_type=F32)

    o_ref[...] = jnp.dot(u_ref[...], w_ref[...],
                         preferred_element_type=F32).astype(o_ref.dtype)


def _gain_spec(d, layer):
    return pl.BlockSpec((None, 1, d), lambda *_: (layer, 0, 0))


def _gains(g):
    return g.reshape(g.shape[0], 1, g.shape[1])


def norm_matmul(h, g, layer, w, w_layer, *, out_dtype, name):
    t, d = h.shape
    n = w.shape[2]
    tm, tn = PROJ_TM, PROJ_TN
    return pl.pallas_call(
        _norm_matmul_kernel,
        grid=(t // tm, n // tn),
        in_specs=[pl.BlockSpec((tm, d), lambda i, j: (i, 0)),
                  _gain_spec(d, layer),
                  pl.BlockSpec((None, d, tn), lambda i, j: (w_layer, 0, j))],
        out_specs=pl.BlockSpec((tm, tn), lambda i, j: (i, j)),
        out_shape=jax.ShapeDtypeStruct((t, n), out_dtype),
        scratch_shapes=[pltpu.VMEM((tm, d), BF16)],
        compiler_params=_params(("parallel", "arbitrary")),
        name=name,
    )(h, _gains(g), w)


def matmul_dt(u, w, w_layer, w_dt_t, *, n_out, out_dtype, name):
    t, d = u.shape
    n_dt = w_dt_t.shape[0]
    tm, tn = PROJ_TM, PROJ_TN
    return pl.pallas_call(
        _matmul_dt_kernel,
        grid=(t // tm, n_out // tn),
        in_specs=[pl.BlockSpec((tm, d), lambda i, j: (i, 0)),
                  pl.BlockSpec((None, d, tn), lambda i, j: (w_layer, 0, j)),
                  pl.BlockSpec((n_dt, d), lambda i, j: (0, 0))],
        out_specs=[pl.BlockSpec((tm, tn), lambda i, j: (i, j)),
                   pl.BlockSpec((n_dt, tm), lambda i, j: (0, i))],
        out_shape=[jax.ShapeDtypeStruct((t, n_out), out_dtype),
                   jax.ShapeDtypeStruct((n_dt, t), F32)],
        compiler_params=_params(("parallel", "arbitrary")),
        name=name,
    )(u, w, w_dt_t)


def _residual_norm(h, m, g_post, g_next=None):
    hn = h + m * _rms_scale(m) * g_post
    if g_next is None:
        return hn, None
    return hn, (hn * _rms_scale(hn) * g_next).astype(BF16)


def _lagged_residual_steps(accumulate, acc_ref, h_ref, gpost_ref, gnext_ref, o_ref, un_ref,
                           n_slices):
    i = pl.program_id(0)
    j = pl.program_id(1)
    last = pl.num_programs(0) - 1
    slice_rows = acc_ref.shape[1] // n_slices

    def add(part):
        slot = i % 2
        acc_ref[slot] = jnp.where(j == 0, part, acc_ref[slot] + part)

    def finish_slice():
        rows = pl.ds(pl.multiple_of(j * slice_rows, slice_rows), slice_rows)
        prev = (i + 1) % 2
        m = acc_ref[prev, rows, :]
        hn, un = _residual_norm(h_ref[...], m, gpost_ref[...],
                                None if gnext_ref is None else gnext_ref[...])
        o_ref[...] = hn
        if un_ref is not None:
            un_ref[...] = un

    @pl.when((i == 0) & (j == 0))
    def _():
        acc_ref[...] = jnp.zeros_like(acc_ref)

    @pl.when(i == 0)
    def _():
        accumulate(add)

    @pl.when((i > 0) & (i < last))
    def _():
        finish_slice()
        accumulate(add)

    @pl.when(i == last)
    def _():
        finish_slice()


def _lagged_tile_maps(n_tiles, n_steps):
    tile_in = lambda i, j: jnp.minimum(i, n_tiles - 1)
    slice_out = lambda i, j: jnp.where(i == 0, 0, (i - 1) * n_steps + j)
    step = lambda i, j: jnp.where(i < n_tiles, j, n_steps - 1)
    return tile_in, slice_out, step


def _out_proj_kernel(y_ref, w_ref, h_ref, gpost_ref, gnext_ref, o_ref, un_ref, acc_ref, *, n_k):
    tk = y_ref.shape[1]

    def accumulate(add):
        w_rows = pl.ds(pl.multiple_of(pl.program_id(1) * tk, tk), tk)
        add(jnp.dot(y_ref[...], w_ref[w_rows, :], preferred_element_type=F32))

    _lagged_residual_steps(accumulate, acc_ref, h_ref, gpost_ref, gnext_ref, o_ref, un_ref, n_k)


def out_proj_residual(y, w, w_layer, h, g_post, layer, g_next, next_layer, *, name):
    t, k = y.shape
    d = w.shape[2]
    tm, tk = OUT_TM, OUT_TK
    n, n_k = t // tm, k // tk
    tile_in, slice_out, step = _lagged_tile_maps(n, n_k)
    finished = pl.BlockSpec((tm // n_k, d), lambda i, j: (slice_out(i, j), 0))
    return pl.pallas_call(
        functools.partial(_out_proj_kernel, n_k=n_k),
        grid=(n + 1, n_k),
        in_specs=[pl.BlockSpec((tm, tk), lambda i, j: (tile_in(i, j), step(i, j))),
                  pl.BlockSpec((None, k, d), lambda i, j: (w_layer, 0, 0),
                               pipeline_mode=pl.Buffered(1)),
                  finished,
                  _gain_spec(d, layer),
                  _gain_spec(d, next_layer)],
        out_specs=[finished, finished],
        out_shape=[jax.ShapeDtypeStruct((t, d), F32), jax.ShapeDtypeStruct((t, d), BF16)],
        scratch_shapes=[pltpu.VMEM((2, tm, d), F32)],
        compiler_params=_params(("arbitrary", "arbitrary")),
        name=name,
    )(y, w, h, _gains(g_post), _gains(g_next))


def _mlp_kernel(u_ref, h_ref, wup_ref, wdn_ref, gpost_ref, *rest, n_f, emit_u):
    if emit_u:
        gnext_ref, o_ref, un_ref, acc_ref = rest
    else:
        (o_ref, acc_ref), gnext_ref, un_ref = rest, None, None

    def accumulate(add):
        hid = jnp.maximum(jnp.dot(u_ref[...], wup_ref[...], preferred_element_type=F32), 0.0)
        add(jnp.dot((hid * hid).astype(BF16), wdn_ref[...], preferred_element_type=F32))

    _lagged_residual_steps(accumulate, acc_ref, h_ref, gpost_ref, gnext_ref, o_ref, un_ref, n_f)


def mlp_residual(u, h, wup, wdn, gpost, layer, g_next=None, next_layer=None, *, name):
    t, d = h.shape
    f = wup.shape[2]
    tm, tf = MLP_TM, MLP_TF
    n, n_f = t // tm, f // tf
    emit_u = g_next is not None
    tile_in, slice_out, step = _lagged_tile_maps(n, n_f)
    finished = pl.BlockSpec((tm // n_f, d), lambda i, j: (slice_out(i, j), 0))
    in_specs = [pl.BlockSpec((tm, d), lambda i, j: (tile_in(i, j), 0)),
                finished,
                pl.BlockSpec((None, d, tf), lambda i, j: (layer, 0, step(i, j))),
                pl.BlockSpec((None, tf, d), lambda i, j: (layer, step(i, j), 0)),
                _gain_spec(d, layer)]
    args = [u, h, wup, wdn, _gains(gpost)]
    out_specs = [finished]
    out_shape = [jax.ShapeDtypeStruct((t, d), F32)]
    if emit_u:
        in_specs.append(_gain_spec(d, next_layer))
        args.append(_gains(g_next))
        out_specs.append(finished)
        out_shape.append(jax.ShapeDtypeStruct((t, d), BF16))
    return pl.pallas_call(
        functools.partial(_mlp_kernel, n_f=n_f, emit_u=emit_u),
        grid=(n + 1, n_f),
        in_specs=in_specs,
        out_specs=out_specs,
        out_shape=out_shape,
        scratch_shapes=[pltpu.VMEM((2, tm, d), F32)],
        compiler_params=_params(("arbitrary", "arbitrary")),
        name=name,
    )(*args)


def _retention_kernel(tab_ref, q_ref, k_ref, v_ref, g_ref, cos_ref, sin_ref, gnw_ref,
                      o_ref, state_ref, dmask_ref, xi_ref, zeta_ref, *, n_heads, blk):
    head = pl.program_id(1)
    half = RET_HEAD_DK // 2
    log_gamma = tab_ref[head]
    gamma_blk = tab_ref[n_heads + head]

    @pl.when(pl.program_id(2) == 0)
    def _():
        state_ref[...] = jnp.zeros_like(state_ref)
        ri = lax.broadcasted_iota(jnp.int32, (blk, blk), 0)
        ci = lax.broadcasted_iota(jnp.int32, (blk, blk), 1)
        dist = jnp.abs(ri - ci).astype(F32)
        shift = int(math.log2(CHUNK))
        visible = jnp.right_shift(ci, shift) <= jnp.right_shift(ri, shift)
        dmask_ref[...] = jnp.where(visible, jnp.exp(dist * log_gamma), 0.0)
        pos = lax.broadcasted_iota(jnp.int32, (blk, V7X_LANES), 0).astype(F32)
        xi_ref[...] = jnp.exp((pos + 1.0) * log_gamma)
        zeta_ref[...] = jnp.exp((blk - 1.0 - pos) * log_gamma)

    xi = jnp.concatenate([xi_ref[...]] * (RET_HEAD_DV // V7X_LANES), axis=1)
    zeta = jnp.concatenate([zeta_ref[...]] * (RET_HEAD_DK // V7X_LANES), axis=1)
    gnw = gnw_ref[...]

    for sub in range(q_ref.shape[0] // blk):
        rows = pl.ds(sub * blk, blk)
        cos = cos_ref[rows, :]
        sin = sin_ref[rows, :]

        def rotary(t):
            t1, t2 = t[:, :half], t[:, half:]
            return jnp.concatenate([t1 * cos - t2 * sin, t1 * sin + t2 * cos], axis=1)

        qr = rotary(q_ref[rows, :].astype(F32))
        kr = rotary(k_ref[rows, :].astype(F32)) * (RET_HEAD_DK ** -0.5)
        qb = qr.astype(BF16)
        kb = kr.astype(BF16)
        vb = v_ref[rows, :].astype(BF16)

        scores = lax.dot_general(qb, kb, (((1,), (1,)), ((), ())), preferred_element_type=F32)
        inner = jnp.dot((scores * dmask_ref[...]).astype(BF16), vb, preferred_element_type=F32)
        state = state_ref[...]
        cross = jnp.dot(qb, state.astype(BF16), preferred_element_type=F32)
        o = inner + cross * xi

        kz = (kr * zeta).astype(BF16)
        upd = lax.dot_general(kz, vb, (((0,), (0,)), ((), ())), preferred_element_type=F32)
        state_ref[...] = state * gamma_blk + upd

        mu = jnp.mean(o, axis=-1, keepdims=True)
        dev = o - mu
        var = jnp.mean(dev * dev, axis=-1, keepdims=True)
        normed = dev * lax.rsqrt(var + GN_EPS) * gnw
        o_ref[rows, :] = (_silu(g_ref[rows, :].astype(F32)) * normed).astype(o_ref.dtype)


def retention_core(proj, gn_w, *, batch, seq, name):
    t = proj.shape[0]
    n_heads = proj.shape[1] // (2 * RET_HEAD_DK + 2 * RET_HEAD_DV)
    step, blk = RET_STEP, RET_BLK
    steps = seq // step
    dv_per_dk = RET_HEAD_DV // RET_HEAD_DK
    k_off = n_heads
    v_off = 2 * n_heads // dv_per_dk
    g_off = v_off + n_heads

    half = RET_HEAD_DK // 2
    inv_freq = ROPE_BASE ** (-jnp.arange(half, dtype=F32) / half)
    ang = jnp.arange(seq).astype(F32)[:, None] * inv_freq[None, :]
    cos, sin = jnp.cos(ang), jnp.sin(ang)
    log_gamma = jnp.log1p(-jnp.exp2(-5.0 - jnp.arange(n_heads, dtype=F32)))
    tab = jnp.concatenate([log_gamma, jnp.exp(blk * log_gamma)])

    row = lambda b, h, c: b * steps + c
    return pl.pallas_call(
        functools.partial(_retention_kernel, n_heads=n_heads, blk=blk),
        grid=(batch, n_heads, steps),
        in_specs=[pl.BlockSpec(memory_space=pltpu.SMEM),
                  pl.BlockSpec((step, RET_HEAD_DK), lambda b, h, c: (row(b, h, c), h)),
                  pl.BlockSpec((step, RET_HEAD_DK), lambda b, h, c: (row(b, h, c), k_off + h)),
                  pl.BlockSpec((step, RET_HEAD_DV), lambda b, h, c: (row(b, h, c), v_off + h)),
                  pl.BlockSpec((step, RET_HEAD_DV), lambda b, h, c: (row(b, h, c), g_off + h)),
                  pl.BlockSpec((step, half), lambda b, h, c: (c, 0)),
                  pl.BlockSpec((step, half), lambda b, h, c: (c, 0)),
                  pl.BlockSpec((1, RET_HEAD_DV), lambda b, h, c: (0, h))],
        out_specs=pl.BlockSpec((step, RET_HEAD_DV), lambda b, h, c: (row(b, h, c), h)),
        out_shape=jax.ShapeDtypeStruct((t, n_heads * RET_HEAD_DV), BF16),
        scratch_shapes=[pltpu.VMEM((RET_HEAD_DK, RET_HEAD_DV), F32),
                        pltpu.VMEM((blk, blk), F32),
                        pltpu.VMEM((blk, V7X_LANES), F32),
                        pltpu.VMEM((blk, V7X_LANES), F32)],
        compiler_params=_params(("parallel", "parallel", "arbitrary")),
        name=name,
    )(tab, proj, proj, proj, proj, cos, sin, gn_w.reshape(1, -1))


def _softplus(x):
    return jnp.maximum(x, 0.0) + jnp.log1p(jnp.exp(-jnp.abs(x)))


def _split_bf16(v, terms):
    parts = []
    for _ in range(terms - 1):
        part = v.astype(BF16)
        parts.append(part)
        v = v - part.astype(F32)
    parts.append(v.astype(BF16))
    return jnp.concatenate(parts, axis=0)


def _sum_terms(stacked, rows):
    out = stacked[0:rows]
    for i in range(1, stacked.shape[0] // rows):
        out = out + stacked[i * rows:(i + 1) * rows]
    return out


def _ssd_decay_kernel(dt_ref, prm_ref, src_ref, acum_ref, inw_ref, *, blk):
    prm = prm_ref[...]
    a2 = -jnp.exp(prm[:, 1:2]) * LOG2_E
    ri = lax.broadcasted_iota(jnp.int32, (blk, blk), 0)
    ci = lax.broadcasted_iota(jnp.int32, (blk, blk), 1)
    upper_ones = jnp.where(ri <= ci, 1.0, 0.0).astype(BF16)
    n_heads = dt_ref.shape[0]
    for sub in range(dt_ref.shape[1] // blk):
        cols = pl.ds(sub * blk, blk)
        dt = _softplus(dt_ref[:, cols] + prm[:, 0:1])
        acum = _sum_terms(jnp.dot(_split_bf16(dt * a2, 3), upper_ones,
                                  preferred_element_type=F32), n_heads)
        src_ref[:, cols] = acum - jnp.log2(dt)
        acum_ref[:, cols] = acum
        inw_ref[:, cols] = dt * jnp.exp2(acum[:, blk - 1:blk] - acum)


def ssd_decays(dt_raw_t, dt_bias, a_log, *, blk, name):
    n_heads, t = dt_raw_t.shape
    step = SSD_DECAY_STEP
    spec = pl.BlockSpec((n_heads, step), lambda i: (0, i))
    shape = jax.ShapeDtypeStruct((n_heads, t), F32)
    return pl.pallas_call(
        functools.partial(_ssd_decay_kernel, blk=blk),
        grid=(t // step,),
        in_specs=[spec, pl.BlockSpec((n_heads, 2), lambda i: (0, 0))],
        out_specs=[spec, spec, spec],
        out_shape=[shape, shape, shape],
        compiler_params=_params(("parallel",)),
        name=name,
    )(dt_raw_t, jnp.stack([dt_bias, a_log], axis=-1))


def _ssd_kernel(z_ref, x_ref, b_ref, c_ref, wx_ref, wb_ref, wc_ref, bx_ref, bb_ref, bc_ref,
                src_ref, acum_ref, inw_ref, dexp_ref, nw_ref, o_ref,
                state_ref, xe_ref, be_ref, ce_ref, *, blk):
    step_rows = x_ref.shape[0]
    halo = V7X_SUBLANES

    @pl.when(pl.program_id(2) == 0)
    def _():
        state_ref[...] = jnp.zeros_like(state_ref)
        xe_ref[0:halo, :] = jnp.zeros((halo, xe_ref.shape[1]), F32)
        be_ref[0:halo, :] = jnp.zeros((halo, be_ref.shape[1]), F32)
        ce_ref[0:halo, :] = jnp.zeros((halo, ce_ref.shape[1]), F32)

    xe_ref[halo:, :] = x_ref[...].astype(F32)
    be_ref[halo:, :] = b_ref[...].astype(F32)
    ce_ref[halo:, :] = c_ref[...].astype(F32)

    def conv_silu(ext_ref, w_ref, bias_ref, off):
        out = bias_ref[...] + ext_ref[pl.ds(halo + off, blk), :] * w_ref[SSD_CONV_W - 1:SSD_CONV_W, :]
        for s in range(1, SSD_CONV_W):
            out = out + (ext_ref[pl.ds(halo + off - s, blk), :]
                         * w_ref[SSD_CONV_W - 1 - s:SSD_CONV_W - s, :])
        return _silu(out)

    ri = lax.broadcasted_iota(jnp.int32, (blk, blk), 0)
    ci = lax.broadcasted_iota(jnp.int32, (blk, blk), 1)
    causal = ri >= ci
    lane = lax.broadcasted_iota(jnp.int32, (blk, V7X_LANES), 1)
    first = lane < SSD_HEADDIM
    keep_first = jnp.where(first, 1.0, 0.0).astype(BF16)
    keep_second = jnp.where(first, 0.0, 1.0).astype(BF16)
    dexp = dexp_ref[...]
    nw = nw_ref[...]
    n_heads = acum_ref.shape[0]
    gw = x_ref.shape[1]

    def head_selector(terms, lanes_per_head):
        k = lax.broadcasted_iota(jnp.int32, (terms * n_heads, n_heads * lanes_per_head), 0)
        n = lax.broadcasted_iota(jnp.int32, (terms * n_heads, n_heads * lanes_per_head), 1)
        shift = int(math.log2(lanes_per_head))
        return jnp.where((k & (n_heads - 1)) == jnp.right_shift(n, shift), 1.0, 0.0).astype(BF16)

    sel_mask = head_selector(3, blk)
    sel_chan = head_selector(2, SSD_HEADDIM)
    transposed_lhs = (((0,), (0,)), ((), ()))

    for sub in range(step_rows // blk):
        off = sub * blk
        src_r = src_ref[:, pl.ds(off, blk)]
        acum_r = acum_ref[:, pl.ds(off, blk)]
        acum_bc = lax.dot_general(_split_bf16(acum_r, 3), sel_mask, transposed_lhs,
                                  preferred_element_type=F32)
        in_weight = lax.dot_general(_split_bf16(inw_ref[:, pl.ds(off, blk)], 2), sel_chan,
                                    transposed_lhs, preferred_element_type=F32)
        xs = conv_silu(xe_ref, wx_ref, bx_ref, off)
        bmb = conv_silu(be_ref, wb_ref, bb_ref, off).astype(BF16)
        cmb = conv_silu(ce_ref, wc_ref, bc_ref, off).astype(BF16)

        cb = lax.dot_general(cmb, bmb, (((1,), (1,)), ((), ())), preferred_element_type=F32)
        state = state_ref[...]
        y_off = jnp.dot(cmb, state.astype(BF16), preferred_element_type=F32)

        y_tiles, decay_tiles = [], []
        for pair in range(gw // V7X_LANES):
            r0, r1 = 2 * pair, 2 * pair + 1
            cols = slice(pair * V7X_LANES, (pair + 1) * V7X_LANES)
            xp = xs[:, cols]
            xpb = xp.astype(BF16)
            out_decay = jnp.exp2(jnp.where(first, acum_bc[:, r0 * blk:r0 * blk + V7X_LANES],
                                           acum_bc[:, r1 * blk:r1 * blk + V7X_LANES]))
            y = y_off[:, cols] * out_decay + dexp[:, cols] * xp
            for r, keep in ((r0, keep_first), (r1, keep_second)):
                seg = acum_bc[:, r * blk:(r + 1) * blk] - src_r[r:r + 1, :]
                lmat = jnp.exp2(jnp.where(causal, seg, -jnp.inf))
                y = y + jnp.dot((cb * lmat).astype(BF16), xpb * keep,
                                preferred_element_type=F32)
            y_tiles.append(y)
            decay_tiles.append(out_decay[blk - 1:blk, :])
        y = jnp.concatenate(y_tiles, axis=1)
        xw = (xs * in_weight).astype(BF16)
        upd = lax.dot_general(bmb, xw, transposed_lhs, preferred_element_type=F32)
        state_ref[...] = state * jnp.concatenate(decay_tiles, axis=1) + upd

        yg = y * _silu(z_ref[pl.ds(off, blk), :].astype(F32))
        o_ref[pl.ds(off, blk), :] = (yg * _rms_scale(yg) * nw).astype(o_ref.dtype)

    xe_ref[0:halo, :] = xe_ref[step_rows:step_rows + halo, :]
    be_ref[0:halo, :] = be_ref[step_rows:step_rows + halo, :]
    ce_ref[0:halo, :] = ce_ref[step_rows:step_rows + halo, :]


def ssd_core(zxbc, dt_raw_t, conv_w, conv_b, dt_bias, a_log, d_skip, norm_w, *, batch, seq, name):
    t = zxbc.shape[0]
    g, r, n, gw = SSD_GROUPS, SSD_HEADS_PER_GROUP, SSD_STATE, SSD_GROUP_WIDTH
    d_inner = g * gw
    step, blk = SSD_STEP, SSD_BLK
    steps = seq // step
    x_off = d_inner // gw
    b_off = 2 * d_inner // n
    c_off = b_off + g
    wb_off = d_inner // n
    wc_off = wb_off + g

    src_r, acum_r, inw_r = (a.reshape(g, r, t) for a in
                           ssd_decays(dt_raw_t, dt_bias, a_log, blk=blk, name=name + "_decays"))
    dexp = jnp.repeat(d_skip, SSD_HEADDIM).reshape(1, d_inner)
    conv_b = conv_b.reshape(1, -1)

    row = lambda b, gi, c: b * steps + c
    return pl.pallas_call(
        functools.partial(_ssd_kernel, blk=blk),
        grid=(batch, g, steps),
        in_specs=[pl.BlockSpec((step, gw), lambda b, gi, c: (row(b, gi, c), gi)),
                  pl.BlockSpec((step, gw), lambda b, gi, c: (row(b, gi, c), x_off + gi)),
                  pl.BlockSpec((step, n), lambda b, gi, c: (row(b, gi, c), b_off + gi)),
                  pl.BlockSpec((step, n), lambda b, gi, c: (row(b, gi, c), c_off + gi)),
                  pl.BlockSpec((SSD_CONV_W, gw), lambda b, gi, c: (0, gi)),
                  pl.BlockSpec((SSD_CONV_W, n), lambda b, gi, c: (0, wb_off + gi)),
                  pl.BlockSpec((SSD_CONV_W, n), lambda b, gi, c: (0, wc_off + gi)),
                  pl.BlockSpec((1, gw), lambda b, gi, c: (0, gi)),
                  pl.BlockSpec((1, n), lambda b, gi, c: (0, wb_off + gi)),
                  pl.BlockSpec((1, n), lambda b, gi, c: (0, wc_off + gi)),
                  pl.BlockSpec((None, r, step), lambda b, gi, c: (gi, 0, row(b, gi, c))),
                  pl.BlockSpec((None, r, step), lambda b, gi, c: (gi, 0, row(b, gi, c))),
                  pl.BlockSpec((None, r, step), lambda b, gi, c: (gi, 0, row(b, gi, c))),
                  pl.BlockSpec((1, gw), lambda b, gi, c: (0, gi)),
                  pl.BlockSpec((1, gw), lambda b, gi, c: (0, gi))],
        out_specs=pl.BlockSpec((step, gw), lambda b, gi, c: (row(b, gi, c), gi)),
        out_shape=jax.ShapeDtypeStruct((t, d_inner), BF16),
        scratch_shapes=[pltpu.VMEM((n, gw), F32),
                        pltpu.VMEM((step + V7X_SUBLANES, gw), F32),
                        pltpu.VMEM((step + V7X_SUBLANES, n), F32),
                        pltpu.VMEM((step + V7X_SUBLANES, n), F32)],
        compiler_params=_params(("parallel", "parallel", "arbitrary")),
        name=name,
    )(zxbc, zxbc, zxbc, zxbc, conv_w, conv_w, conv_w, conv_b, conv_b, conv_b,
      src_r, acum_r, inw_r, dexp, norm_w.reshape(1, d_inner))


def kernel(x, norm_mix_pre, norm_mix_post, norm_ffn_pre, norm_ffn_post, ret_w_in, ret_gn_w, ret_w_out, ssd_w_in, ssd_conv_w, ssd_conv_b, ssd_dt_bias, ssd_a_log, ssd_d, ssd_norm_w, ssd_w_out, mlp_w_up, mlp_w_down):
    batch, seq, d = x.shape
    h = x.reshape(batch * seq, d)
    act_dtype = F32

    n_zxbc = ssd_w_in.shape[2] - SSD_GROUPS * SSD_HEADS_PER_GROUP
    w_dt_t = ssd_w_in[0, :, n_zxbc:].T.astype(BF16)
    ret_w_in, ret_w_out, ssd_w_in, ssd_w_out, mlp_w_up, mlp_w_down = (
        w.astype(BF16) for w in (ret_w_in, ret_w_out, ssd_w_in, ssd_w_out, mlp_w_up, mlp_w_down))

    proj = norm_matmul(h, norm_mix_pre, 0, ret_w_in, 0, out_dtype=act_dtype, name="ret_in_proj")
    y = retention_core(proj, ret_gn_w[0], batch=batch, seq=seq, name="retention_core")
    h, u = out_proj_residual(y, ret_w_out, 0, h, norm_mix_post, 0, norm_ffn_pre, 0,
                             name="ret_out_proj")
    h, u = mlp_residual(u, h, mlp_w_up, mlp_w_down, norm_ffn_post, 0, norm_mix_pre, 1, name="mlp0")

    zxbc, dt_raw_t = matmul_dt(u, ssd_w_in, 0, w_dt_t, n_out=n_zxbc, out_dtype=act_dtype,
                               name="ssd_in_proj")
    y = ssd_core(zxbc, dt_raw_t, ssd_conv_w[0], ssd_conv_b[0], ssd_dt_bias[0], ssd_a_log[0],
                 ssd_d[0], ssd_norm_w[0], batch=batch, seq=seq, name="ssd_core")
    h, u = out_proj_residual(y, ssd_w_out, 0, h, norm_mix_post, 1, norm_ffn_pre, 1,
                             name="ssd_out_proj")
    (h,) = mlp_residual(u, h, mlp_w_up, mlp_w_down, norm_ffn_post, 1, name="mlp1")
    return h.reshape(batch, seq, d)
```

```python
import functools
import math

import jax
import jax.numpy as jnp
from jax import lax
from jax.experimental import pallas as pl
from jax.experimental.pallas import tpu as pltpu

F32 = jnp.float32
BF16 = jnp.bfloat16

RMS_EPS = 1e-6
GN_EPS = 1e-5
ROPE_BASE = 10000.0
CHUNK = 64
LOG2_E = math.log2(math.e)

RET_HEAD_DK = 256
RET_HEAD_DV = 512
SSD_HEADDIM = 64
SSD_GROUPS = 8
SSD_HEADS_PER_GROUP = 8
SSD_STATE = 128
SSD_CONV_W = 4
SSD_GROUP_WIDTH = SSD_HEADS_PER_GROUP * SSD_HEADDIM

V7X_LANES = 128
V7X_SUBLANES = 8
V7X_VMEM_LIMIT_BYTES = 56 * 1024 * 1024

PROJ_TM, PROJ_TN = 1024, 2048
PROJ_WCAST_TN = 1024
OUT_TM, OUT_TK = 512, 2048
MLP_TM, MLP_TF = 1024, 1024
RET_STEP, RET_BLK = 1024, 256
SSD_DECAY_STEP = 2048
SSD_STEP, SSD_BLK = 1024, 128


def _params(semantics):
    return pltpu.CompilerParams(dimension_semantics=semantics,
                                vmem_limit_bytes=V7X_VMEM_LIMIT_BYTES)


def _rms_scale(x):
    return lax.rsqrt(jnp.mean(x * x, axis=-1, keepdims=True) + RMS_EPS)


def _silu(x):
    half = 0.5 * x
    return half + half * jnp.tanh(half)


def _norm_matmul_kernel(x_ref, g_ref, w_ref, o_ref, u_ref):
    @pl.when(pl.program_id(1) == 0)
    def _():
        x = x_ref[...]
        u_ref[...] = (x * _rms_scale(x) * g_ref[...]).astype(BF16)

    o_ref[...] = jnp.dot(u_ref[...], w_ref[...],
                         preferred_element_type=F32).astype(o_ref.dtype)


def _matmul_dt_kernel(u_ref, wt_ref, wdt_ref, o_ref, dt_ref, wb_ref):
    contract_last = (((1,), (1,)), ((), ()))

    @pl.when(pl.program_id(1) == 0)
    def _():
        wb_ref[...] = wt_ref[...].astype(BF16)

    @pl.when(pl.program_id(0) == 0)
    def _():
        dt_ref[...] = lax.dot_general(wdt_ref[...].astype(BF16), u_ref[...], contract_last,
                                      preferred_element_type=F32)

    o_ref[...] = lax.dot_general(u_ref[...], wb_ref[...], contract_last,
                                 preferred_element_type=F32).astype(o_ref.dtype)


def _gain_spec(d, layer):
    return pl.BlockSpec((None, 1, d), lambda *_: (layer, 0, 0))


def _gains(g):
    return g.reshape(g.shape[0], 1, g.shape[1])


def norm_matmul(h, g, layer, w, w_layer, *, out_dtype, name):
    t, d = h.shape
    n = w.shape[2]
    tm, tn = PROJ_TM, PROJ_TN
    return pl.pallas_call(
        _norm_matmul_kernel,
        grid=(t // tm, n // tn),
        in_specs=[pl.BlockSpec((tm, d), lambda i, j: (i, 0)),
                  _gain_spec(d, layer),
                  pl.BlockSpec((None, d, tn), lambda i, j: (w_layer, 0, j))],
        out_specs=pl.BlockSpec((tm, tn), lambda i, j: (i, j)),
        out_shape=jax.ShapeDtypeStruct((t, n), out_dtype),
        scratch_shapes=[pltpu.VMEM((tm, d), BF16)],
        compiler_params=_params(("parallel", "arbitrary")),
        name=name,
    )(h, _gains(g), w)


def matmul_dt(u, w_t, w_layer, *, n_out, out_dtype, name):
    t, d = u.shape
    n_dt = w_t.shape[1] - n_out
    tm, tn = PROJ_TM, PROJ_WCAST_TN
    n_rows = t // tm
    dt_rows = n_out // n_dt
    dt_block = lambda j, i: (0, jnp.where(j == 0, i, n_rows - 1))
    return pl.pallas_call(
        _matmul_dt_kernel,
        grid=(n_out // tn, n_rows),
        in_specs=[pl.BlockSpec((tm, d), lambda j, i: (i, 0)),
                  pl.BlockSpec((None, tn, d), lambda j, i: (w_layer, j, 0)),
                  pl.BlockSpec((None, n_dt, d), lambda j, i: (w_layer, dt_rows, 0))],
        out_specs=[pl.BlockSpec((tm, tn), lambda j, i: (i, j)),
                   pl.BlockSpec((n_dt, tm), dt_block)],
        out_shape=[jax.ShapeDtypeStruct((t, n_out), out_dtype),
                   jax.ShapeDtypeStruct((n_dt, t), F32)],
        scratch_shapes=[pltpu.VMEM((tn, d), BF16)],
        compiler_params=_params(("arbitrary", "arbitrary")),
        name=name,
    )(u, w_t, w_t)


def _residual_norm(h, m, g_post, g_next=None):
    hn = h + m * _rms_scale(m) * g_post
    if g_next is None:
        return hn, None
    return hn, (hn * _rms_scale(hn) * g_next).astype(BF16)


def _lagged_residual_steps(accumulate, acc_ref, h_ref, gpost_ref, gnext_ref, o_ref, un_ref,
                           n_slices):
    i = pl.program_id(0)
    j = pl.program_id(1)
    last = pl.num_programs(0) - 1
    slice_rows = acc_ref.shape[1] // n_slices

    def add(part):
        slot = i % 2
        acc_ref[slot] = jnp.where(j == 0, part, acc_ref[slot] + part)

    def finish_slice():
        rows = pl.ds(pl.multiple_of(j * slice_rows, slice_rows), slice_rows)
        prev = (i + 1) % 2
        m = acc_ref[prev, rows, :]
        hn, un = _residual_norm(h_ref[...], m, gpost_ref[...],
                                None if gnext_ref is None else gnext_ref[...])
        o_ref[...] = hn
        if un_ref is not None:
            un_ref[...] = un

    @pl.when((i == 0) & (j == 0))
    def _():
        acc_ref[...] = jnp.zeros_like(acc_ref)

    @pl.when(i == 0)
    def _():
        accumulate(add)

    @pl.when((i > 0) & (i < last))
    def _():
        finish_slice()
        accumulate(add)

    @pl.when(i == last)
    def _():
        finish_slice()


def _lagged_tile_maps(n_tiles, n_steps):
    tile_in = lambda i, j: jnp.minimum(i, n_tiles - 1)
    slice_out = lambda i, j: jnp.where(i == 0, 0, (i - 1) * n_steps + j)
    step = lambda i, j: jnp.where(i < n_tiles, j, n_steps - 1)
    return tile_in, slice_out, step


def _out_proj_kernel(y_ref, w_ref, h_ref, gpost_ref, gnext_ref, o_ref, un_ref, acc_ref, *, n_k):
    tk = y_ref.shape[1]

    def accumulate(add):
        w_rows = pl.ds(pl.multiple_of(pl.program_id(1) * tk, tk), tk)
        add(jnp.dot(y_ref[...], w_ref[w_rows, :], preferred_element_type=F32))

    _lagged_residual_steps(accumulate, acc_ref, h_ref, gpost_ref, gnext_ref, o_ref, un_ref, n_k)


def out_proj_residual(y, w, w_layer, h, g_post, layer, g_next, next_layer, *, name):
    t, k = y.shape
    d = w.shape[2]
    tm, tk = OUT_TM, OUT_TK
    n, n_k = t // tm, k // tk
    tile_in, slice_out, step = _lagged_tile_maps(n, n_k)
    finished = pl.BlockSpec((tm // n_k, d), lambda i, j: (slice_out(i, j), 0))
    return pl.pallas_call(
        functools.partial(_out_proj_kernel, n_k=n_k),
        grid=(n + 1, n_k),
        in_specs=[pl.BlockSpec((tm, tk), lambda i, j: (tile_in(i, j), step(i, j))),
                  pl.BlockSpec((None, k, d), lambda i, j: (w_layer, 0, 0),
                               pipeline_mode=pl.Buffered(1)),
                  finished,
                  _gain_spec(d, layer),
                  _gain_spec(d, next_layer)],
        out_specs=[finished, finished],
        out_shape=[jax.ShapeDtypeStruct((t, d), F32), jax.ShapeDtypeStruct((t, d), BF16)],
        scratch_shapes=[pltpu.VMEM((2, tm, d), F32)],
        compiler_params=_params(("arbitrary", "arbitrary")),
        name=name,
    )(y, w, h, _gains(g_post), _gains(g_next))


def _mlp_kernel(u_ref, h_ref, wup_ref, wdn_ref, gpost_ref, *rest, n_f, emit_u):
    if emit_u:
        gnext_ref, o_ref, un_ref, acc_ref = rest
    else:
        (o_ref, acc_ref), gnext_ref, un_ref = rest, None, None

    def accumulate(add):
        hid = jnp.maximum(jnp.dot(u_ref[...], wup_ref[...], preferred_element_type=F32), 0.0)
        add(jnp.dot((hid * hid).astype(BF16), wdn_ref[...], preferred_element_type=F32))

    _lagged_residual_steps(accumulate, acc_ref, h_ref, gpost_ref, gnext_ref, o_ref, un_ref, n_f)


def mlp_residual(u, h, wup, wdn, gpost, layer, g_next=None, next_layer=None, *, name):
    t, d = h.shape
    f = wup.shape[2]
    tm, tf = MLP_TM, MLP_TF
    n, n_f = t // tm, f // tf
    emit_u = g_next is not None
    tile_in, slice_out, step = _lagged_tile_maps(n, n_f)
    finished = pl.BlockSpec((tm // n_f, d), lambda i, j: (slice_out(i, j), 0))
    in_specs = [pl.BlockSpec((tm, d), lambda i, j: (tile_in(i, j), 0)),
                finished,
                pl.BlockSpec((None, d, tf), lambda i, j: (layer, 0, step(i, j))),
                pl.BlockSpec((None, tf, d), lambda i, j: (layer, step(i, j), 0)),
                _gain_spec(d, layer)]
    args = [u, h, wup, wdn, _gains(gpost)]
    out_specs = [finished]
    out_shape = [jax.ShapeDtypeStruct((t, d), F32)]
    if emit_u:
        in_specs.append(_gain_spec(d, next_layer))
        args.append(_gains(g_next))
        out_specs.append(finished)
        out_shape.append(jax.ShapeDtypeStruct((t, d), BF16))
    return pl.pallas_call(
        functools.partial(_mlp_kernel, n_f=n_f, emit_u=emit_u),
        grid=(n + 1, n_f),
        in_specs=in_specs,
        out_specs=out_specs,
        out_shape=out_shape,
        scratch_shapes=[pltpu.VMEM((2, tm, d), F32)],
        compiler_params=_params(("arbitrary", "arbitrary")),
        name=name,
    )(*args)


def _retention_kernel(tab_ref, q_ref, k_ref, v_ref, g_ref, cos_ref, sin_ref, gnw_ref,
                      o_ref, state_ref, dmask_ref, xi_ref, zeta_ref, *, n_heads, blk):
    head = pl.program_id(1)
    half = RET_HEAD_DK // 2
    log_gamma = tab_ref[head]
    gamma_blk = tab_ref[n_heads + head]

    @pl.when(pl.program_id(2) == 0)
    def _():
        state_ref[...] = jnp.zeros_like(state_ref)
        ri = lax.broadcasted_iota(jnp.int32, (blk, blk), 0)
        ci = lax.broadcasted_iota(jnp.int32, (blk, blk), 1)
        dist = jnp.abs(ri - ci).astype(F32)
        shift = int(math.log2(CHUNK))
        visible = jnp.right_shift(ci, shift) <= jnp.right_shift(ri, shift)
        dmask_ref[...] = jnp.where(visible, jnp.exp(dist * log_gamma), 0.0)
        pos = lax.broadcasted_iota(jnp.int32, (blk, V7X_LANES), 0).astype(F32)
        xi_ref[...] = jnp.exp((pos + 1.0) * log_gamma)
        zeta_ref[...] = jnp.exp((blk - 1.0 - pos) * log_gamma)

    xi = jnp.concatenate([xi_ref[...]] * (RET_HEAD_DV // V7X_LANES), axis=1)
    zeta = jnp.concatenate([zeta_ref[...]] * (RET_HEAD_DK // V7X_LANES), axis=1)
    gnw = gnw_ref[...]

    for sub in range(q_ref.shape[0] // blk):
        rows = pl.ds(sub * blk, blk)
        cos = cos_ref[rows, :]
        sin = sin_ref[rows, :]

        def rotary(t):
            t1, t2 = t[:, :half], t[:, half:]
            return jnp.concatenate([t1 * cos - t2 * sin, t1 * sin + t2 * cos], axis=1)

        qr = rotary(q_ref[rows, :].astype(F32))
        kr = rotary(k_ref[rows, :].astype(F32)) * (RET_HEAD_DK ** -0.5)
        qb = qr.astype(BF16)
        kb = kr.astype(BF16)
        vb = v_ref[rows, :].astype(BF16)

        scores = lax.dot_general(qb, kb, (((1,), (1,)), ((), ())), preferred_element_type=F32)
        inner = jnp.dot((scores * dmask_ref[...]).astype(BF16), vb, preferred_element_type=F32)
        state = state_ref[...]
        cross = jnp.dot(qb, state.astype(BF16), preferred_element_type=F32)
        o = inner + cross * xi

        kz = (kr * zeta).astype(BF16)
        upd = lax.dot_general(kz, vb, (((0,), (0,)), ((), ())), preferred_element_type=F32)
        state_ref[...] = state * gamma_blk + upd

        mu = jnp.mean(o, axis=-1, keepdims=True)
        dev = o - mu
        var = jnp.mean(dev * dev, axis=-1, keepdims=True)
        normed = dev * lax.rsqrt(var + GN_EPS) * gnw
        o_ref[rows, :] = (_silu(g_ref[rows, :].astype(F32)) * normed).astype(o_ref.dtype)


def retention_core(proj, gn_w, *, batch, seq, name):
    t = proj.shape[0]
    n_heads = proj.shape[1] // (2 * RET_HEAD_DK + 2 * RET_HEAD_DV)
    step, blk = RET_STEP, RET_BLK
    steps = seq // step
    dv_per_dk = RET_HEAD_DV // RET_HEAD_DK
    k_off = n_heads
    v_off = 2 * n_heads // dv_per_dk
    g_off = v_off + n_heads

    half = RET_HEAD_DK // 2
    inv_freq = ROPE_BASE ** (-jnp.arange(half, dtype=F32) / half)
    ang = jnp.arange(seq).astype(F32)[:, None] * inv_freq[None, :]
    cos, sin = jnp.cos(ang), jnp.sin(ang)
    log_gamma = jnp.log1p(-jnp.exp2(-5.0 - jnp.arange(n_heads, dtype=F32)))
    tab = jnp.concatenate([log_gamma, jnp.exp(blk * log_gamma)])

    row = lambda b, h, c: b * steps + c
    return pl.pallas_call(
        functools.partial(_retention_kernel, n_heads=n_heads, blk=blk),
        grid=(batch, n_heads, steps),
        in_specs=[pl.BlockSpec(memory_space=pltpu.SMEM),
                  pl.BlockSpec((step, RET_HEAD_DK), lambda b, h, c: (row(b, h, c), h)),
                  pl.BlockSpec((step, RET_HEAD_DK), lambda b, h, c: (row(b, h, c), k_off + h)),
                  pl.BlockSpec((step, RET_HEAD_DV), lambda b, h, c: (row(b, h, c), v_off + h)),
                  pl.BlockSpec((step, RET_HEAD_DV), lambda b, h, c: (row(b, h, c), g_off + h)),
                  pl.BlockSpec((step, half), lambda b, h, c: (c, 0)),
                  pl.BlockSpec((step, half), lambda b, h, c: (c, 0)),
                  pl.BlockSpec((1, RET_HEAD_DV), lambda b, h, c: (0, h))],
        out_specs=pl.BlockSpec((step, RET_HEAD_DV), lambda b, h, c: (row(b, h, c), h)),
        out_shape=jax.ShapeDtypeStruct((t, n_heads * RET_HEAD_DV), BF16),
        scratch_shapes=[pltpu.VMEM((RET_HEAD_DK, RET_HEAD_DV), F32),
                        pltpu.VMEM((blk, blk), F32),
                        pltpu.VMEM((blk, V7X_LANES), F32),
                        pltpu.VMEM((blk, V7X_LANES), F32)],
        compiler_params=_params(("parallel", "parallel", "arbitrary")),
        name=name,
    )(tab, proj, proj, proj, proj, cos, sin, gn_w.reshape(1, -1))


def _softplus(x):
    return jnp.maximum(x, 0.0) + jnp.log1p(jnp.exp(-jnp.abs(x)))


def _split_bf16(v, terms):
    parts = []
    for _ in range(terms - 1):
        part = v.astype(BF16)
        parts.append(part)
        v = v - part.astype(F32)
    parts.append(v.astype(BF16))
    return jnp.concatenate(parts, axis=0)


def _sum_terms(stacked, rows):
    out = stacked[0:rows]
    for i in range(1, stacked.shape[0] // rows):
        out = out + stacked[i * rows:(i + 1) * rows]
    return out


def _ssd_decay_kernel(dt_ref, prm_ref, src_ref, acum_ref, inw_ref, *, blk):
    prm = prm_ref[...]
    a2 = -jnp.exp(prm[:, 1:2]) * LOG2_E
    ri = lax.broadcasted_iota(jnp.int32, (blk, blk), 0)
    ci = lax.broadcasted_iota(jnp.int32, (blk, blk), 1)
    upper_ones = jnp.where(ri <= ci, 1.0, 0.0).astype(BF16)
    n_heads = dt_ref.shape[0]
    for sub in range(dt_ref.shape[1] // blk):
        cols = pl.ds(sub * blk, blk)
        dt = _softplus(dt_ref[:, cols] + prm[:, 0:1])
        acum = _sum_terms(jnp.dot(_split_bf16(dt * a2, 3), upper_ones,
                                  preferred_element_type=F32), n_heads)
        src_ref[:, cols] = acum - jnp.log2(dt)
        acum_ref[:, cols] = acum
        inw_ref[:, cols] = dt * jnp.exp2(acum[:, blk - 1:blk] - acum)


def ssd_decays(dt_raw_t, dt_bias, a_log, *, blk, name):
    n_heads, t = dt_raw_t.shape
    step = SSD_DECAY_STEP
    spec = pl.BlockSpec((n_heads, step), lambda i: (0, i))
    shape = jax.ShapeDtypeStruct((n_heads, t), F32)
    return pl.pallas_call(
        functools.partial(_ssd_decay_kernel, blk=blk),
        grid=(t // step,),
        in_specs=[spec, pl.BlockSpec((n_heads, 2), lambda i: (0, 0))],
        out_specs=[spec, spec, spec],
        out_shape=[shape, shape, shape],
        compiler_params=_params(("parallel",)),
        name=name,
    )(dt_raw_t, jnp.stack([dt_bias, a_log], axis=-1))


def _ssd_kernel(z_ref, x_ref, b_ref, c_ref, wx_ref, wb_ref, wc_ref, bx_ref, bb_ref, bc_ref,
                src_ref, acum_ref, inw_ref, dexp_ref, nw_ref, o_ref,
                state_ref, xe_ref, be_ref, ce_ref, *, blk):
    step_rows = x_ref.shape[0]
    halo = V7X_SUBLANES

    @pl.when(pl.program_id(2) == 0)
    def _():
        state_ref[...] = jnp.zeros_like(state_ref)
        xe_ref[0:halo, :] = jnp.zeros((halo, xe_ref.shape[1]), F32)
        be_ref[0:halo, :] = jnp.zeros((halo, be_ref.shape[1]), F32)
        ce_ref[0:halo, :] = jnp.zeros((halo, ce_ref.shape[1]), F32)

    xe_ref[halo:, :] = x_ref[...].astype(F32)
    be_ref[halo:, :] = b_ref[...].astype(F32)
    ce_ref[halo:, :] = c_ref[...].astype(F32)

    def conv_silu(ext_ref, w_ref, bias_ref, off):
        out = bias_ref[...] + ext_ref[pl.ds(halo + off, blk), :] * w_ref[SSD_CONV_W - 1:SSD_CONV_W, :]
        for s in range(1, SSD_CONV_W):
            out = out + (ext_ref[pl.ds(halo + off - s, blk), :]
                         * w_ref[SSD_CONV_W - 1 - s:SSD_CONV_W - s, :])
        return _silu(out)

    ri = lax.broadcasted_iota(jnp.int32, (blk, blk), 0)
    ci = lax.broadcasted_iota(jnp.int32, (blk, blk), 1)
    causal = ri >= ci
    lane = lax.broadcasted_iota(jnp.int32, (blk, V7X_LANES), 1)
    first = lane < SSD_HEADDIM
    keep_first = jnp.where(first, 1.0, 0.0).astype(BF16)
    keep_second = jnp.where(first, 0.0, 1.0).astype(BF16)
    dexp = dexp_ref[...]
    nw = nw_ref[...]
    n_heads = acum_ref.shape[0]
    gw = x_ref.shape[1]

    def head_selector(terms, lanes_per_head):
        k = lax.broadcasted_iota(jnp.int32, (terms * n_heads, n_heads * lanes_per_head), 0)
        n = lax.broadcasted_iota(jnp.int32, (terms * n_heads, n_heads * lanes_per_head), 1)
        shift = int(math.log2(lanes_per_head))
        return jnp.where((k & (n_heads - 1)) == jnp.right_shift(n, shift), 1.0, 0.0).astype(BF16)

    sel_mask = head_selector(3, blk)
    sel_chan = head_selector(2, SSD_HEADDIM)
    transposed_lhs = (((0,), (0,)), ((), ()))

    for sub in range(step_rows // blk):
        off = sub * blk
        src_r = src_ref[:, pl.ds(off, blk)]
        acum_r = acum_ref[:, pl.ds(off, blk)]
        acum_bc = lax.dot_general(_split_bf16(acum_r, 3), sel_mask, transposed_lhs,
                                  preferred_element_type=F32)
        in_weight = lax.dot_general(_split_bf16(inw_ref[:, pl.ds(off, blk)], 2), sel_chan,
                                    transposed_lhs, preferred_element_type=F32)
        xs = conv_silu(xe_ref, wx_ref, bx_ref, off)
        bmb = conv_silu(be_ref, wb_ref, bb_ref, off).astype(BF16)
        cmb = conv_silu(ce_ref, wc_ref, bc_ref, off).astype(BF16)

        cb = lax.dot_general(cmb, bmb, (((1,), (1,)), ((), ())), preferred_element_type=F32)
        state = state_ref[...]
        y_off = jnp.dot(cmb, state.astype(BF16), preferred_element_type=F32)

        y_tiles, decay_tiles = [], []
        for pair in range(gw // V7X_LANES):
            r0, r1 = 2 * pair, 2 * pair + 1
            cols = slice(pair * V7X_LANES, (pair + 1) * V7X_LANES)
            xp = xs[:, cols]
            xpb = xp.astype(BF16)
            out_decay = jnp.exp2(jnp.where(first, acum_bc[:, r0 * blk:r0 * blk + V7X_LANES],
                                           acum_bc[:, r1 * blk:r1 * blk + V7X_LANES]))
            y = y_off[:, cols] * out_decay + dexp[:, cols] * xp
            for r, keep in ((r0, keep_first), (r1, keep_second)):
                seg = acum_bc[:, r * blk:(r + 1) * blk] - src_r[r:r + 1, :]
                lmat = jnp.exp2(jnp.where(causal, seg, -jnp.inf))
                y = y + jnp.dot((cb * lmat).astype(BF16), xpb * keep,
                                preferred_element_type=F32)
            y_tiles.append(y)
            decay_tiles.append(out_decay[blk - 1:blk, :])
        y = jnp.concatenate(y_tiles, axis=1)
        xw = (xs * in_weight).astype(BF16)
        upd = lax.dot_general(bmb, xw, transposed_lhs, preferred_element_type=F32)
        state_ref[...] = state * jnp.concatenate(decay_tiles, axis=1) + upd

        yg = y * _silu(z_ref[pl.ds(off, blk), :].astype(F32))
        o_ref[pl.ds(off, blk), :] = (yg * _rms_scale(yg) * nw).astype(o_ref.dtype)

    xe_ref[0:halo, :] = xe_ref[step_rows:step_rows + halo, :]
    be_ref[0:halo, :] = be_ref[step_rows:step_rows + halo, :]
    ce_ref[0:halo, :] = ce_ref[step_rows:step_rows + halo, :]


def ssd_core(zxbc, dt_raw_t, conv_w, conv_b, dt_bias, a_log, d_skip, norm_w, *, batch, seq, name):
    t = zxbc.shape[0]
    g, r, n, gw = SSD_GROUPS, SSD_HEADS_PER_GROUP, SSD_STATE, SSD_GROUP_WIDTH
    d_inner = g * gw
    step, blk = SSD_STEP, SSD_BLK
    steps = seq // step
    x_off = d_inner // gw
    b_off = 2 * d_inner // n
    c_off = b_off + g
    wb_off = d_inner // n
    wc_off = wb_off + g

    src_r, acum_r, inw_r = (a.reshape(g, r, t) for a in
                           ssd_decays(dt_raw_t, dt_bias, a_log, blk=blk, name=name + "_decays"))
    dexp = jnp.repeat(d_skip, SSD_HEADDIM).reshape(1, d_inner)
    conv_b = conv_b.reshape(1, -1)

    row = lambda b, gi, c: b * steps + c
    return pl.pallas_call(
        functools.partial(_ssd_kernel, blk=blk),
        grid=(batch, g, steps),
        in_specs=[pl.BlockSpec((step, gw), lambda b, gi, c: (row(b, gi, c), gi)),
                  pl.BlockSpec((step, gw), lambda b, gi, c: (row(b, gi, c), x_off + gi)),
                  pl.BlockSpec((step, n), lambda b, gi, c: (row(b, gi, c), b_off + gi)),
                  pl.BlockSpec((step, n), lambda b, gi, c: (row(b, gi, c), c_off + gi)),
                  pl.BlockSpec((SSD_CONV_W, gw), lambda b, gi, c: (0, gi)),
                  pl.BlockSpec((SSD_CONV_W, n), lambda b, gi, c: (0, wb_off + gi)),
                  pl.BlockSpec((SSD_CONV_W, n), lambda b, gi, c: (0, wc_off + gi)),
                  pl.BlockSpec((1, gw), lambda b, gi, c: (0, gi)),
                  pl.BlockSpec((1, n), lambda b, gi, c: (0, wb_off + gi)),
                  pl.BlockSpec((1, n), lambda b, gi, c: (0, wc_off + gi)),
                  pl.BlockSpec((None, r, step), lambda b, gi, c: (gi, 0, row(b, gi, c))),
                  pl.BlockSpec((None, r, step), lambda b, gi, c: (gi, 0, row(b, gi, c))),
                  pl.BlockSpec((None, r, step), lambda b, gi, c: (gi, 0, row(b, gi, c))),
                  pl.BlockSpec((1, gw), lambda b, gi, c: (0, gi)),
                  pl.BlockSpec((1, gw), lambda b, gi, c: (0, gi))],
        out_specs=pl.BlockSpec((step, gw), lambda b, gi, c: (row(b, gi, c), gi)),
        out_shape=jax.ShapeDtypeStruct((t, d_inner), BF16),
        scratch_shapes=[pltpu.VMEM((n, gw), F32),
                        pltpu.VMEM((step + V7X_SUBLANES, gw), F32),
                        pltpu.VMEM((step + V7X_SUBLANES, n), F32),
                        pltpu.VMEM((step + V7X_SUBLANES, n), F32)],
        compiler_params=_params(("parallel", "parallel", "arbitrary")),
        name=name,
    )(zxbc, zxbc, zxbc, zxbc, conv_w, conv_w, conv_w, conv_b, conv_b, conv_b,
      src_r, acum_r, inw_r, dexp, norm_w.reshape(1, d_inner))


def kernel(x, norm_mix_pre, norm_mix_post, norm_ffn_pre, norm_ffn_post, ret_w_in, ret_gn_w, ret_w_out, ssd_w_in, ssd_conv_w, ssd_conv_b, ssd_dt_bias, ssd_a_log, ssd_d, ssd_norm_w, ssd_w_out, mlp_w_up, mlp_w_down):
    batch, seq, d = x.shape
    h = x.reshape(batch * seq, d)
    act_dtype = F32

    n_zxbc = ssd_w_in.shape[2] - SSD_GROUPS * SSD_HEADS_PER_GROUP
    ssd_w_in_t = jnp.swapaxes(ssd_w_in, 1, 2)
    ret_w_in, ret_w_out, ssd_w_out, mlp_w_up, mlp_w_down = (
        w.astype(BF16) for w in (ret_w_in, ret_w_out, ssd_w_out, mlp_w_up, mlp_w_down))

    proj = norm_matmul(h, norm_mix_pre, 0, ret_w_in, 0, out_dtype=act_dtype, name="ret_in_proj")
    y = retention_core(proj, ret_gn_w[0], batch=batch, seq=seq, name="retention_core")
    h, u = out_proj_residual(y, ret_w_out, 0, h, norm_mix_post, 0, norm_ffn_pre, 0,
                             name="ret_out_proj")
    h, u = mlp_residual(u, h, mlp_w_up, mlp_w_down, norm_ffn_post, 0, norm_mix_pre, 1, name="mlp0")

    zxbc, dt_raw_t = matmul_dt(u, ssd_w_in_t, 0, n_out=n_zxbc, out_dtype=act_dtype,
                               name="ssd_in_proj")
    y = ssd_core(zxbc, dt_raw_t, ssd_conv_w[0], ssd_conv_b[0], ssd_dt_bias[0], ssd_a_log[0],
                 ssd_d[0], ssd_norm_w[0], batch=batch, seq=seq, name="ssd_core")
    h, u = out_proj_residual(y, ssd_w_out, 0, h, norm_mix_post, 1, norm_ffn_pre, 1,
                             name="ssd_out_proj")
    (h,) = mlp_residual(u, h, mlp_w_up, mlp_w_down, norm_ffn_post, 1, name="mlp1")
    return h.reshape(batch, seq, d)
```

```python
import functools
import math

import jax
import jax.numpy as jnp
from jax import lax
from jax.experimental import pallas as pl
from jax.experimental.pallas import tpu as pltpu

F32 = jnp.float32
BF16 = jnp.bfloat16

RMS_EPS = 1e-6
GN_EPS = 1e-5
ROPE_BASE = 10000.0
CHUNK = 64
LOG2_E = math.log2(math.e)

RET_HEAD_DK = 256
RET_HEAD_DV = 512
SSD_HEADDIM = 64
SSD_GROUPS = 8
SSD_HEADS_PER_GROUP = 8
SSD_STATE = 128
SSD_CONV_W = 4
SSD_GROUP_WIDTH = SSD_HEADS_PER_GROUP * SSD_HEADDIM

V7X_LANES = 128
V7X_SUBLANES = 8
V7X_VMEM_LIMIT_BYTES = 56 * 1024 * 1024

PROJ_TM, PROJ_TN = 1024, 2048
PROJ_WCAST_TN = 1024
OUT_TM, OUT_TK = 512, 2048
MLP_TM, MLP_TF = 1024, 1024
RET_STEP, RET_BLK = 1024, 256
SSD_DECAY_STEP = 2048
SSD_STEP, SSD_BLK = 1024, 128


def _params(semantics):
    return pltpu.CompilerParams(dimension_semantics=semantics,
                                vmem_limit_bytes=V7X_VMEM_LIMIT_BYTES)


def _rms_scale(x):
    return lax.rsqrt(jnp.mean(x * x, axis=-1, keepdims=True) + RMS_EPS)


def _silu(x):
    half = 0.5 * x
    return half + half * jnp.tanh(half)


def _norm_matmul_kernel(x_ref, g_ref, w_ref, o_ref, u_ref):
    @pl.when(pl.program_id(1) == 0)
    def _():
        x = x_ref[...]
        u_ref[...] = (x * _rms_scale(x) * g_ref[...]).astype(BF16)

    o_ref[...] = jnp.dot(u_ref[...], w_ref[...],
                         preferred_element_type=F32).astype(o_ref.dtype)


def _matmul_dt_kernel(u_ref, wt_ref, wdt_ref, o_ref, dt_ref, wb_ref):
    contract_last = (((1,), (1,)), ((), ()))

    @pl.when(pl.program_id(1) == 0)
    def _():
        wb_ref[...] = wt_ref[...].astype(BF16)

    @pl.when(pl.program_id(0) == 0)
    def _():
        dt_ref[...] = lax.dot_general(wdt_ref[...].astype(BF16), u_ref[...], contract_last,
                                      preferred_element_type=F32)

    o_ref[...] = lax.dot_general(u_ref[...], wb_ref[...], contract_last,
                                 preferred_element_type=F32).astype(o_ref.dtype)


def _gain_spec(d, layer):
    return pl.BlockSpec((None, 1, d), lambda *_: (layer, 0, 0))


def _gains(g):
    return g.reshape(g.shape[0], 1, g.shape[1])


def norm_matmul(h, g, layer, w, w_layer, *, out_dtype, name):
    t, d = h.shape
    n = w.shape[2]
    tm, tn = PROJ_TM, PROJ_TN
    return pl.pallas_call(
        _norm_matmul_kernel,
        grid=(t // tm, n // tn),
        in_specs=[pl.BlockSpec((tm, d), lambda i, j: (i, 0)),
                  _gain_spec(d, layer),
                  pl.BlockSpec((None, d, tn), lambda i, j: (w_layer, 0, j))],
        out_specs=pl.BlockSpec((tm, tn), lambda i, j: (i, j)),
        out_shape=jax.ShapeDtypeStruct((t, n), out_dtype),
        scratch_shapes=[pltpu.VMEM((tm, d), BF16)],
        compiler_params=_params(("parallel", "arbitrary")),
        name=name,
    )(h, _gains(g), w)


def matmul_dt(u, w_t, w_layer, *, n_out, out_dtype, name):
    t, d = u.shape
    n_dt = w_t.shape[1] - n_out
    tm, tn = PROJ_TM, PROJ_WCAST_TN
    n_rows = t // tm
    dt_rows = n_out // n_dt
    dt_block = lambda j, i: (0, jnp.where(j == 0, i, n_rows - 1))
    return pl.pallas_call(
        _matmul_dt_kernel,
        grid=(n_out // tn, n_rows),
        in_specs=[pl.BlockSpec((tm, d), lambda j, i: (i, 0)),
                  pl.BlockSpec((None, tn, d), lambda j, i: (w_layer, j, 0)),
                  pl.BlockSpec((None, n_dt, d), lambda j, i: (w_layer, dt_rows, 0))],
        out_specs=[pl.BlockSpec((tm, tn), lambda j, i: (i, j)),
                   pl.BlockSpec((n_dt, tm), dt_block)],
        out_shape=[jax.ShapeDtypeStruct((t, n_out), out_dtype),
                   jax.ShapeDtypeStruct((n_dt, t), F32)],
        scratch_shapes=[pltpu.VMEM((tn, d), BF16)],
        compiler_params=_params(("arbitrary", "arbitrary")),
        name=name,
    )(u, w_t, w_t)


def _residual_norm(h, m, g_post, g_next=None):
    hn = h + m * _rms_scale(m) * g_post
    if g_next is None:
        return hn, None
    return hn, (hn * _rms_scale(hn) * g_next).astype(BF16)


def _lagged_residual_steps(accumulate, acc_ref, h_ref, gpost_ref, gnext_ref, o_ref, un_ref,
                           n_slices):
    i = pl.program_id(0)
    j = pl.program_id(1)
    last = pl.num_programs(0) - 1
    slice_rows = acc_ref.shape[1] // n_slices

    def add(part):
        slot = i % 2
        acc_ref[slot] = jnp.where(j == 0, part, acc_ref[slot] + part)

    def finish_slice():
        rows = pl.ds(pl.multiple_of(j * slice_rows, slice_rows), slice_rows)
        prev = (i + 1) % 2
        m = acc_ref[prev, rows, :]
        hn, un = _residual_norm(h_ref[...], m, gpost_ref[...],
                                None if gnext_ref is None else gnext_ref[...])
        o_ref[...] = hn
        if un_ref is not None:
            un_ref[...] = un

    @pl.when((i == 0) & (j == 0))
    def _():
        acc_ref[...] = jnp.zeros_like(acc_ref)

    @pl.when(i == 0)
    def _():
        accumulate(add)

    @pl.when((i > 0) & (i < last))
    def _():
        finish_slice()
        accumulate(add)

    @pl.when(i == last)
    def _():
        finish_slice()


def _lagged_tile_maps(n_tiles, n_steps):
    tile_in = lambda i, j: jnp.minimum(i, n_tiles - 1)
    slice_out = lambda i, j: jnp.where(i == 0, 0, (i - 1) * n_steps + j)
    step = lambda i, j: jnp.where(i < n_tiles, j, n_steps - 1)
    return tile_in, slice_out, step


def _out_proj_kernel(y_ref, w_ref, h_ref, gpost_ref, gnext_ref, o_ref, un_ref, acc_ref, *, n_k):
    tk = y_ref.shape[1]

    def accumulate(add):
        w_rows = pl.ds(pl.multiple_of(pl.program_id(1) * tk, tk), tk)
        add(jnp.dot(y_ref[...], w_ref[w_rows, :], preferred_element_type=F32))

    _lagged_residual_steps(accumulate, acc_ref, h_ref, gpost_ref, gnext_ref, o_ref, un_ref, n_k)


def out_proj_residual(y, w, w_layer, h, g_post, layer, g_next, next_layer, *, name):
    t, k = y.shape
    d = w.shape[2]
    tm, tk = OUT_TM, OUT_TK
    n, n_k = t // tm, k // tk
    tile_in, slice_out, step = _lagged_tile_maps(n, n_k)
    finished = pl.BlockSpec((tm // n_k, d), lambda i, j: (slice_out(i, j), 0))
    return pl.pallas_call(
        functools.partial(_out_proj_kernel, n_k=n_k),
        grid=(n + 1, n_k),
        in_specs=[pl.BlockSpec((tm, tk), lambda i, j: (tile_in(i, j), step(i, j))),
                  pl.BlockSpec((None, k, d), lambda i, j: (w_layer, 0, 0),
                               pipeline_mode=pl.Buffered(1)),
                  finished,
                  _gain_spec(d, layer),
                  _gain_spec(d, next_layer)],
        out_specs=[finished, finished],
        out_shape=[jax.ShapeDtypeStruct((t, d), F32), jax.ShapeDtypeStruct((t, d), BF16)],
        scratch_shapes=[pltpu.VMEM((2, tm, d), F32)],
        compiler_params=_params(("arbitrary", "arbitrary")),
        name=name,
    )(y, w, h, _gains(g_post), _gains(g_next))


def _mlp_kernel(u_ref, h_ref, wup_ref, wdn_ref, gpost_ref, *rest, n_f, emit_u):
    if emit_u:
        gnext_ref, o_ref, un_ref, acc_ref = rest
    else:
        (o_ref, acc_ref), gnext_ref, un_ref = rest, None, None

    def accumulate(add):
        hid = jnp.maximum(jnp.dot(u_ref[...], wup_ref[...], preferred_element_type=F32), 0.0)
        add(jnp.dot((hid * hid).astype(BF16), wdn_ref[...], preferred_element_type=F32))

    _lagged_residual_steps(accumulate, acc_ref, h_ref, gpost_ref, gnext_ref, o_ref, un_ref, n_f)


def mlp_residual(u, h, wup, wdn, gpost, layer, g_next=None, next_layer=None, *, name):
    t, d = h.shape
    f = wup.shape[2]
    tm, tf = MLP_TM, MLP_TF
    n, n_f = t // tm, f // tf
    emit_u = g_next is not None
    tile_in, slice_out, step = _lagged_tile_maps(n, n_f)
    finished = pl.BlockSpec((tm // n_f, d), lambda i, j: (slice_out(i, j), 0))
    in_specs = [pl.BlockSpec((tm, d), lambda i, j: (tile_in(i, j), 0)),
                finished,
                pl.BlockSpec((None, d, tf), lambda i, j: (layer, 0, step(i, j))),
                pl.BlockSpec((None, tf, d), lambda i, j: (layer, step(i, j), 0)),
                _gain_spec(d, layer)]
    args = [u, h, wup, wdn, _gains(gpost)]
    out_specs = [finished]
    out_shape = [jax.ShapeDtypeStruct((t, d), F32)]
    if emit_u:
        in_specs.append(_gain_spec(d, next_layer))
        args.append(_gains(g_next))
        out_specs.append(finished)
        out_shape.append(jax.ShapeDtypeStruct((t, d), BF16))
    return pl.pallas_call(
        functools.partial(_mlp_kernel, n_f=n_f, emit_u=emit_u),
        grid=(n + 1, n_f),
        in_specs=in_specs,
        out_specs=out_specs,
        out_shape=out_shape,
        scratch_shapes=[pltpu.VMEM((2, tm, d), F32)],
        compiler_params=_params(("arbitrary", "arbitrary")),
        name=name,
    )(*args)


def _retention_kernel(tab_ref, q_ref, k_ref, v_ref, g_ref, cos_ref, sin_ref, gnw_ref,
                      o_ref, state_ref, dmask_ref, xi_ref, zeta_ref, *, n_heads, blk):
    head = pl.program_id(2)
    half = RET_HEAD_DK // 2
    log_gamma = tab_ref[head]
    gamma_blk = tab_ref[n_heads + head]

    @pl.when(pl.program_id(1) == 0)
    def _():
        state_ref[head] = jnp.zeros(state_ref.shape[1:], F32)
        ri = lax.broadcasted_iota(jnp.int32, (blk, blk), 0)
        ci = lax.broadcasted_iota(jnp.int32, (blk, blk), 1)
        dist = jnp.abs(ri - ci).astype(F32)
        shift = int(math.log2(CHUNK))
        visible = jnp.right_shift(ci, shift) <= jnp.right_shift(ri, shift)
        dmask_ref[head] = jnp.where(visible, jnp.exp(dist * log_gamma), 0.0)
        pos = lax.broadcasted_iota(jnp.int32, (blk, V7X_LANES), 0).astype(F32)
        xi_ref[head] = jnp.exp((pos + 1.0) * log_gamma)
        zeta_ref[head] = jnp.exp((blk - 1.0 - pos) * log_gamma)

    xi = jnp.concatenate([xi_ref[head]] * (RET_HEAD_DV // V7X_LANES), axis=1)
    zeta = jnp.concatenate([zeta_ref[head]] * (RET_HEAD_DK // V7X_LANES), axis=1)
    dmask = dmask_ref[head]
    gnw = gnw_ref[...]

    for sub in range(q_ref.shape[0] // blk):
        rows = pl.ds(sub * blk, blk)
        cos = cos_ref[rows, :]
        sin = sin_ref[rows, :]

        def rotary(t):
            t1, t2 = t[:, :half], t[:, half:]
            return jnp.concatenate([t1 * cos - t2 * sin, t1 * sin + t2 * cos], axis=1)

        qr = rotary(q_ref[rows, :].astype(F32))
        kr = rotary(k_ref[rows, :].astype(F32)) * (RET_HEAD_DK ** -0.5)
        qb = qr.astype(BF16)
        kb = kr.astype(BF16)
        vb = v_ref[rows, :].astype(BF16)

        scores = lax.dot_general(qb, kb, (((1,), (1,)), ((), ())), preferred_element_type=F32)
        inner = jnp.dot((scores * dmask).astype(BF16), vb, preferred_element_type=F32)
        state = state_ref[head]
        cross = jnp.dot(qb, state.astype(BF16), preferred_element_type=F32)
        o = inner + cross * xi

        kz = (kr * zeta).astype(BF16)
        upd = lax.dot_general(kz, vb, (((0,), (0,)), ((), ())), preferred_element_type=F32)
        state_ref[head] = state * gamma_blk + upd

        mu = jnp.mean(o, axis=-1, keepdims=True)
        dev = o - mu
        var = jnp.mean(dev * dev, axis=-1, keepdims=True)
        normed = dev * lax.rsqrt(var + GN_EPS) * gnw
        o_ref[rows, :] = (_silu(g_ref[rows, :].astype(F32)) * normed).astype(o_ref.dtype)


def retention_core(proj, gn_w, *, batch, seq, name):
    t = proj.shape[0]
    n_heads = proj.shape[1] // (2 * RET_HEAD_DK + 2 * RET_HEAD_DV)
    step, blk = RET_STEP, RET_BLK
    steps = seq // step
    dv_per_dk = RET_HEAD_DV // RET_HEAD_DK
    k_off = n_heads
    v_off = 2 * n_heads // dv_per_dk
    g_off = v_off + n_heads

    half = RET_HEAD_DK // 2
    inv_freq = ROPE_BASE ** (-jnp.arange(half, dtype=F32) / half)
    ang = jnp.arange(seq).astype(F32)[:, None] * inv_freq[None, :]
    cos, sin = jnp.cos(ang), jnp.sin(ang)
    log_gamma = jnp.log1p(-jnp.exp2(-5.0 - jnp.arange(n_heads, dtype=F32)))
    tab = jnp.concatenate([log_gamma, jnp.exp(blk * log_gamma)])

    row = lambda b, c, h: b * steps + c
    return pl.pallas_call(
        functools.partial(_retention_kernel, n_heads=n_heads, blk=blk),
        grid=(batch, steps, n_heads),
        in_specs=[pl.BlockSpec(memory_space=pltpu.SMEM),
                  pl.BlockSpec((step, RET_HEAD_DK), lambda b, c, h: (row(b, c, h), h)),
                  pl.BlockSpec((step, RET_HEAD_DK), lambda b, c, h: (row(b, c, h), k_off + h)),
                  pl.BlockSpec((step, RET_HEAD_DV), lambda b, c, h: (row(b, c, h), v_off + h)),
                  pl.BlockSpec((step, RET_HEAD_DV), lambda b, c, h: (row(b, c, h), g_off + h)),
                  pl.BlockSpec((step, half), lambda b, c, h: (c, 0)),
                  pl.BlockSpec((step, half), lambda b, c, h: (c, 0)),
                  pl.BlockSpec((1, RET_HEAD_DV), lambda b, c, h: (0, h))],
        out_specs=pl.BlockSpec((step, RET_HEAD_DV), lambda b, c, h: (row(b, c, h), h)),
        out_shape=jax.ShapeDtypeStruct((t, n_heads * RET_HEAD_DV), BF16),
        scratch_shapes=[pltpu.VMEM((n_heads, RET_HEAD_DK, RET_HEAD_DV), F32),
                        pltpu.VMEM((n_heads, blk, blk), F32),
                        pltpu.VMEM((n_heads, blk, V7X_LANES), F32),
                        pltpu.VMEM((n_heads, blk, V7X_LANES), F32)],
        compiler_params=_params(("parallel", "arbitrary", "arbitrary")),
        name=name,
    )(tab, proj, proj, proj, proj, cos, sin, gn_w.reshape(1, -1))


def _softplus(x):
    return jnp.maximum(x, 0.0) + jnp.log1p(jnp.exp(-jnp.abs(x)))


def _split_bf16(v, terms):
    parts = []
    for _ in range(terms - 1):
        part = v.astype(BF16)
        parts.append(part)
        v = v - part.astype(F32)
    parts.append(v.astype(BF16))
    return jnp.concatenate(parts, axis=0)


def _sum_terms(stacked, rows):
    out = stacked[0:rows]
    for i in range(1, stacked.shape[0] // rows):
        out = out + stacked[i * rows:(i + 1) * rows]
    return out


def _ssd_decay_kernel(dt_ref, prm_ref, src_ref, acum_ref, inw_ref, *, blk):
    prm = prm_ref[...]
    a2 = -jnp.exp(prm[:, 1:2]) * LOG2_E
    ri = lax.broadcasted_iota(jnp.int32, (blk, blk), 0)
    ci = lax.broadcasted_iota(jnp.int32, (blk, blk), 1)
    upper_ones = jnp.where(ri <= ci, 1.0, 0.0).astype(BF16)
    n_heads = dt_ref.shape[0]
    for sub in range(dt_ref.shape[1] // blk):
        cols = pl.ds(sub * blk, blk)
        dt = _softplus(dt_ref[:, cols] + prm[:, 0:1])
        acum = _sum_terms(jnp.dot(_split_bf16(dt * a2, 3), upper_ones,
                                  preferred_element_type=F32), n_heads)
        src_ref[:, cols] = acum - jnp.log2(dt)
        acum_ref[:, cols] = acum
        inw_ref[:, cols] = dt * jnp.exp2(acum[:, blk - 1:blk] - acum)


def ssd_decays(dt_raw_t, dt_bias, a_log, *, blk, name):
    n_heads, t = dt_raw_t.shape
    step = SSD_DECAY_STEP
    spec = pl.BlockSpec((n_heads, step), lambda i: (0, i))
    shape = jax.ShapeDtypeStruct((n_heads, t), F32)
    return pl.pallas_call(
        functools.partial(_ssd_decay_kernel, blk=blk),
        grid=(t // step,),
        in_specs=[spec, pl.BlockSpec((n_heads, 2), lambda i: (0, 0))],
        out_specs=[spec, spec, spec],
        out_shape=[shape, shape, shape],
        compiler_params=_params(("parallel",)),
        name=name,
    )(dt_raw_t, jnp.stack([dt_bias, a_log], axis=-1))


def _ssd_kernel(z_ref, x_ref, b_ref, c_ref, wx_ref, wb_ref, wc_ref, bx_ref, bb_ref, bc_ref,
                src_ref, acum_ref, inw_ref, dexp_ref, nw_ref, o_ref,
                state_ref, xe_ref, be_ref, ce_ref, *, blk):
    step_rows = x_ref.shape[0]
    halo = V7X_SUBLANES

    @pl.when(pl.program_id(2) == 0)
    def _():
        state_ref[...] = jnp.zeros_like(state_ref)
        xe_ref[0:halo, :] = jnp.zeros((halo, xe_ref.shape[1]), F32)
        be_ref[0:halo, :] = jnp.zeros((halo, be_ref.shape[1]), F32)
        ce_ref[0:halo, :] = jnp.zeros((halo, ce_ref.shape[1]), F32)

    xe_ref[halo:, :] = x_ref[...].astype(F32)
    be_ref[halo:, :] = b_ref[...].astype(F32)
    ce_ref[halo:, :] = c_ref[...].astype(F32)

    def conv_silu(ext_ref, w_ref, bias_ref, off):
        out = bias_ref[...] + ext_ref[pl.ds(halo + off, blk), :] * w_ref[SSD_CONV_W - 1:SSD_CONV_W, :]
        for s in range(1, SSD_CONV_W):
            out = out + (ext_ref[pl.ds(halo + off - s, blk), :]
                         * w_ref[SSD_CONV_W - 1 - s:SSD_CONV_W - s, :])
        return _silu(out)

    ri = lax.broadcasted_iota(jnp.int32, (blk, blk), 0)
    ci = lax.broadcasted_iota(jnp.int32, (blk, blk), 1)
    causal = ri >= ci
    lane = lax.broadcasted_iota(jnp.int32, (blk, V7X_LANES), 1)
    first = lane < SSD_HEADDIM
    keep_first = jnp.where(first, 1.0, 0.0).astype(BF16)
    keep_second = jnp.where(first, 0.0, 1.0).astype(BF16)
    dexp = dexp_ref[...]
    nw = nw_ref[...]
    n_heads = acum_ref.shape[0]
    gw = x_ref.shape[1]

    def head_selector(terms, lanes_per_head):
        k = lax.broadcasted_iota(jnp.int32, (terms * n_heads, n_heads * lanes_per_head), 0)
        n = lax.broadcasted_iota(jnp.int32, (terms * n_heads, n_heads * lanes_per_head), 1)
        shift = int(math.log2(lanes_per_head))
        return jnp.where((k & (n_heads - 1)) == jnp.right_shift(n, shift), 1.0, 0.0).astype(BF16)

    sel_mask = head_selector(3, blk)
    sel_chan = head_selector(2, SSD_HEADDIM)
    transposed_lhs = (((0,), (0,)), ((), ()))

    for sub in range(step_rows // blk):
        off = sub * blk
        src_r = src_ref[:, pl.ds(off, blk)]
        acum_r = acum_ref[:, pl.ds(off, blk)]
        acum_bc = lax.dot_general(_split_bf16(acum_r, 3), sel_mask, transposed_lhs,
                                  preferred_element_type=F32)
        in_weight = lax.dot_general(_split_bf16(inw_ref[:, pl.ds(off, blk)], 2), sel_chan,
                                    transposed_lhs, preferred_element_type=F32)
        xs = conv_silu(xe_ref, wx_ref, bx_ref, off)
        bmb = conv_silu(be_ref, wb_ref, bb_ref, off).astype(BF16)
        cmb = conv_silu(ce_ref, wc_ref, bc_ref, off).astype(BF16)

        cb = lax.dot_general(cmb, bmb, (((1,), (1,)), ((), ())), preferred_element_type=F32)
        state = state_ref[...]
        y_off = jnp.dot(cmb, state.astype(BF16), preferred_element_type=F32)

        y_tiles, decay_tiles = [], []
        for pair in range(gw // V7X_LANES):
            r0, r1 = 2 * pair, 2 * pair + 1
            cols = slice(pair * V7X_LANES, (pair + 1) * V7X_LANES)
            xp = xs[:, cols]
            xpb = xp.astype(BF16)
            out_decay = jnp.exp2(jnp.where(first, acum_bc[:, r0 * blk:r0 * blk + V7X_LANES],
                                           acum_bc[:, r1 * blk:r1 * blk + V7X_LANES]))
            y = y_off[:, cols] * out_decay + dexp[:, cols] * xp
            for r, keep in ((r0, keep_first), (r1, keep_second)):
                seg = acum_bc[:, r * blk:(r + 1) * blk] - src_r[r:r + 1, :]
                lmat = jnp.exp2(jnp.where(causal, seg, -jnp.inf))
                y = y + jnp.dot((cb * lmat).astype(BF16), xpb * keep,
                                preferred_element_type=F32)
            y_tiles.append(y)
            decay_tiles.append(out_decay[blk - 1:blk, :])
        y = jnp.concatenate(y_tiles, axis=1)
        xw = (xs * in_weight).astype(BF16)
        upd = lax.dot_general(bmb, xw, transposed_lhs, preferred_element_type=F32)
        state_ref[...] = state * jnp.concatenate(decay_tiles, axis=1) + upd

        yg = y * _silu(z_ref[pl.ds(off, blk), :].astype(F32))
        o_ref[pl.ds(off, blk), :] = (yg * _rms_scale(yg) * nw).astype(o_ref.dtype)

    xe_ref[0:halo, :] = xe_ref[step_rows:step_rows + halo, :]
    be_ref[0:halo, :] = be_ref[step_rows:step_rows + halo, :]
    ce_ref[0:halo, :] = ce_ref[step_rows:step_rows + halo, :]


def ssd_core(zxbc, dt_raw_t, conv_w, conv_b, dt_bias, a_log, d_skip, norm_w, *, batch, seq, name):
    t = zxbc.shape[0]
    g, r, n, gw = SSD_GROUPS, SSD_HEADS_PER_GROUP, SSD_STATE, SSD_GROUP_WIDTH
    d_inner = g * gw
    step, blk = SSD_STEP, SSD_BLK
    steps = seq // step
    x_off = d_inner // gw
    b_off = 2 * d_inner // n
    c_off = b_off + g
    wb_off = d_inner // n
    wc_off = wb_off + g

    src_r, acum_r, inw_r = (a.reshape(g, r, t) for a in
                           ssd_decays(dt_raw_t, dt_bias, a_log, blk=blk, name=name + "_decays"))
    dexp = jnp.repeat(d_skip, SSD_HEADDIM).reshape(1, d_inner)
    conv_b = conv_b.reshape(1, -1)

    row = lambda b, gi, c: b * steps + c
    return pl.pallas_call(
        functools.partial(_ssd_kernel, blk=blk),
        grid=(batch, g, steps),
        in_specs=[pl.BlockSpec((step, gw), lambda b, gi, c: (row(b, gi, c), gi)),
                  pl.BlockSpec((step, gw), lambda b, gi, c: (row(b, gi, c), x_off + gi)),
                  pl.BlockSpec((step, n), lambda b, gi, c: (row(b, gi, c), b_off + gi)),
                  pl.BlockSpec((step, n), lambda b, gi, c: (row(b, gi, c), c_off + gi)),
                  pl.BlockSpec((SSD_CONV_W, gw), lambda b, gi, c: (0, gi)),
                  pl.BlockSpec((SSD_CONV_W, n), lambda b, gi, c: (0, wb_off + gi)),
                  pl.BlockSpec((SSD_CONV_W, n), lambda b, gi, c: (0, wc_off + gi)),
                  pl.BlockSpec((1, gw), lambda b, gi, c: (0, gi)),
                  pl.BlockSpec((1, n), lambda b, gi, c: (0, wb_off + gi)),
                  pl.BlockSpec((1, n), lambda b, gi, c: (0, wc_off + gi)),
                  pl.BlockSpec((None, r, step), lambda b, gi, c: (gi, 0, row(b, gi, c))),
                  pl.BlockSpec((None, r, step), lambda b, gi, c: (gi, 0, row(b, gi, c))),
                  pl.BlockSpec((None, r, step), lambda b, gi, c: (gi, 0, row(b, gi, c))),
                  pl.BlockSpec((1, gw), lambda b, gi, c: (0, gi)),
                  pl.BlockSpec((1, gw), lambda b, gi, c: (0, gi))],
        out_specs=pl.BlockSpec((step, gw), lambda b, gi, c: (row(b, gi, c), gi)),
        out_shape=jax.ShapeDtypeStruct((t, d_inner), BF16),
        scratch_shapes=[pltpu.VMEM((n, gw), F32),
                        pltpu.VMEM((step + V7X_SUBLANES, gw), F32),
                        pltpu.VMEM((step + V7X_SUBLANES, n), F32),
                        pltpu.VMEM((step + V7X_SUBLANES, n), F32)],
        compiler_params=_params(("parallel", "parallel", "arbitrary")),
        name=name,
    )(zxbc, zxbc, zxbc, zxbc, conv_w, conv_w, conv_w, conv_b, conv_b, conv_b,
      src_r, acum_r, inw_r, dexp, norm_w.reshape(1, d_inner))


def kernel(x, norm_mix_pre, norm_mix_post, norm_ffn_pre, norm_ffn_post, ret_w_in, ret_gn_w, ret_w_out, ssd_w_in, ssd_conv_w, ssd_conv_b, ssd_dt_bias, ssd_a_log, ssd_d, ssd_norm_w, ssd_w_out, mlp_w_up, mlp_w_down):
    batch, seq, d = x.shape
    h = x.reshape(batch * seq, d)

    n_zxbc = ssd_w_in.shape[2] - SSD_GROUPS * SSD_HEADS_PER_GROUP
    ssd_w_in_t = jnp.swapaxes(ssd_w_in, 1, 2)
    ret_w_in, ret_w_out, ssd_w_out, mlp_w_up, mlp_w_down = (
        w.astype(BF16) for w in (ret_w_in, ret_w_out, ssd_w_out, mlp_w_up, mlp_w_down))

    proj = norm_matmul(h, norm_mix_pre, 0, ret_w_in, 0, out_dtype=BF16, name="ret_in_proj")
    y = retention_core(proj, ret_gn_w[0], batch=batch, seq=seq, name="retention_core")
    h, u = out_proj_residual(y, ret_w_out, 0, h, norm_mix_post, 0, norm_ffn_pre, 0,
                             name="ret_out_proj")
    h, u = mlp_residual(u, h, mlp_w_up, mlp_w_down, norm_ffn_post, 0, norm_mix_pre, 1, name="mlp0")

    zxbc, dt_raw_t = matmul_dt(u, ssd_w_in_t, 0, n_out=n_zxbc, out_dtype=F32,
                               name="ssd_in_proj")
    y = ssd_core(zxbc, dt_raw_t, ssd_conv_w[0], ssd_conv_b[0], ssd_dt_bias[0], ssd_a_log[0],
                 ssd_d[0], ssd_norm_w[0], batch=batch, seq=seq, name="ssd_core")
    h, u = out_proj_residual(y, ssd_w_out, 0, h, norm_mix_post, 1, norm_ffn_pre, 1,
                             name="ssd_out_proj")
    (h,) = mlp_residual(u, h, mlp_w_up, mlp_w_down, norm_ffn_post, 1, name="mlp1")
    return h.reshape(batch, seq, d)
```

```python
import functools
import math

import jax
import jax.numpy as jnp
from jax import lax
from jax.experimental import pallas as pl
from jax.experimental.pallas import tpu as pltpu

F32 = jnp.float32
BF16 = jnp.bfloat16

RMS_EPS = 1e-6
GN_EPS = 1e-5
ROPE_BASE = 10000.0
CHUNK = 64
LOG2_E = math.log2(math.e)

RET_HEAD_DK = 256
RET_HEAD_DV = 512
SSD_HEADDIM = 64
SSD_GROUPS = 8
SSD_HEADS_PER_GROUP = 8
SSD_STATE = 128
SSD_CONV_W = 4
SSD_GROUP_WIDTH = SSD_HEADS_PER_GROUP * SSD_HEADDIM

V7X_LANES = 128
V7X_SUBLANES = 8
V7X_VMEM_LIMIT_BYTES = 56 * 1024 * 1024

PROJ_TM, PROJ_TN = 1024, 2048
PROJ_WCAST_TN = 1024
OUT_TM, OUT_TK = 512, 2048
MLP_TM, MLP_TF = 1024, 1024
MLP_FIRST_TF = 512
RET_STEP, RET_BLK = 1024, 256
SSD_DECAY_STEP = 2048
SSD_STEP, SSD_BLK = 1024, 128


def _params(semantics):
    return pltpu.CompilerParams(dimension_semantics=semantics,
                                vmem_limit_bytes=V7X_VMEM_LIMIT_BYTES)


def _rms_scale(x):
    return lax.rsqrt(jnp.mean(x * x, axis=-1, keepdims=True) + RMS_EPS)


def _silu(x):
    half = 0.5 * x
    return half + half * jnp.tanh(half)


def _norm_matmul_kernel(x_ref, g_ref, w_ref, o_ref, u_ref):
    @pl.when(pl.program_id(1) == 0)
    def _():
        x = x_ref[...]
        u_ref[...] = (x * _rms_scale(x) * g_ref[...]).astype(BF16)

    o_ref[...] = jnp.dot(u_ref[...], w_ref[...],
                         preferred_element_type=F32).astype(o_ref.dtype)


def _matmul_dt_kernel(u_ref, wt_ref, wdt_ref, o_ref, dt_ref, wb_ref):
    contract_last = (((1,), (1,)), ((), ()))

    @pl.when(pl.program_id(1) == 0)
    def _():
        wb_ref[...] = wt_ref[...].astype(BF16)

    @pl.when(pl.program_id(0) == 0)
    def _():
        dt_ref[...] = lax.dot_general(wdt_ref[...].astype(BF16), u_ref[...], contract_last,
                                      preferred_element_type=F32)

    o_ref[...] = lax.dot_general(u_ref[...], wb_ref[...], contract_last,
                                 preferred_element_type=F32).astype(o_ref.dtype)


def _gain_spec(d, layer):
    return pl.BlockSpec((None, 1, d), lambda *_: (layer, 0, 0))


def _gains(g):
    return g.reshape(g.shape[0], 1, g.shape[1])


def norm_matmul(h, g, layer, w, w_layer, *, out_dtype, name):
    t, d = h.shape
    n = w.shape[2]
    tm, tn = PROJ_TM, PROJ_TN
    return pl.pallas_call(
        _norm_matmul_kernel,
        grid=(t // tm, n // tn),
        in_specs=[pl.BlockSpec((tm, d), lambda i, j: (i, 0)),
                  _gain_spec(d, layer),
                  pl.BlockSpec((None, d, tn), lambda i, j: (w_layer, 0, j))],
        out_specs=pl.BlockSpec((tm, tn), lambda i, j: (i, j)),
        out_shape=jax.ShapeDtypeStruct((t, n), out_dtype),
        scratch_shapes=[pltpu.VMEM((tm, d), BF16)],
        compiler_params=_params(("parallel", "arbitrary")),
        name=name,
    )(h, _gains(g), w)


def matmul_dt(u, w_t, w_layer, *, n_out, out_dtype, name):
    t, d = u.shape
    n_dt = w_t.shape[1] - n_out
    tm, tn = PROJ_TM, PROJ_WCAST_TN
    n_rows = t // tm
    dt_rows = n_out // n_dt
    dt_block = lambda j, i: (0, jnp.where(j == 0, i, n_rows - 1))
    return pl.pallas_call(
        _matmul_dt_kernel,
        grid=(n_out // tn, n_rows),
        in_specs=[pl.BlockSpec((tm, d), lambda j, i: (i, 0)),
                  pl.BlockSpec((None, tn, d), lambda j, i: (w_layer, j, 0)),
                  pl.BlockSpec((None, n_dt, d), lambda j, i: (w_layer, dt_rows, 0))],
        out_specs=[pl.BlockSpec((tm, tn), lambda j, i: (i, j)),
                   pl.BlockSpec((n_dt, tm), dt_block)],
        out_shape=[jax.ShapeDtypeStruct((t, n_out), out_dtype),
                   jax.ShapeDtypeStruct((n_dt, t), F32)],
        scratch_shapes=[pltpu.VMEM((tn, d), BF16)],
        compiler_params=_params(("arbitrary", "arbitrary")),
        name=name,
    )(u, w_t, w_t)


def _residual_norm(h, m, g_post, g_next=None):
    hn = h + m * _rms_scale(m) * g_post
    if g_next is None:
        return hn, None
    return hn, (hn * _rms_scale(hn) * g_next).astype(BF16)


def _lagged_residual_steps(accumulate, acc_ref, seed_ref, h_ref, gpost_ref, gnext_ref, o_ref,
                           un_ref, n_slices):
    i = pl.program_id(0)
    j = pl.program_id(1)
    last = pl.num_programs(0) - 1
    first = 0 if seed_ref is None else 1
    slice_rows = acc_ref.shape[1] // n_slices

    def add(part):
        slot = (i + first) % 2
        acc_ref[slot] = jnp.where(j == 0, part, acc_ref[slot] + part)

    def finish_slice(m):
        hn, un = _residual_norm(h_ref[...], m, gpost_ref[...],
                                None if gnext_ref is None else gnext_ref[...])
        o_ref[...] = hn
        if un_ref is not None:
            un_ref[...] = un

    def finish_from_acc():
        rows = pl.ds(pl.multiple_of(j * slice_rows, slice_rows), slice_rows)
        finish_slice(acc_ref[(i + first + 1) % 2, rows, :])

    @pl.when((i == 0) & (j == 0))
    def _():
        acc_ref[...] = jnp.zeros_like(acc_ref)

    @pl.when(i == 0)
    def _():
        if seed_ref is not None:
            finish_slice(seed_ref[...])
        accumulate(add)

    @pl.when((i > 0) & (i < last))
    def _():
        finish_from_acc()
        accumulate(add)

    @pl.when(i == last)
    def _():
        finish_from_acc()


def _lagged_tile_maps(n_tiles, n_steps, seeded):
    first = 1 if seeded else 0
    rows = n_tiles - first + 1
    tile_in = lambda i, j: jnp.minimum(i + first, n_tiles - 1)
    slice_out = lambda i, j: jnp.where(i + first == 0, 0, (i + first - 1) * n_steps + j)
    step = lambda i, j: jnp.where(i < rows - 1, j, n_steps - 1)
    return rows, tile_in, slice_out, step


def _out_proj_kernel(y_ref, w_ref, h_ref, gpost_ref, gnext_ref, o_ref, un_ref, acc_ref, *, n_k):
    tk = y_ref.shape[1]

    def accumulate(add):
        w_rows = pl.ds(pl.multiple_of(pl.program_id(1) * tk, tk), tk)
        add(jnp.dot(y_ref[...], w_ref[w_rows, :], preferred_element_type=F32))

    _lagged_residual_steps(accumulate, acc_ref, None, h_ref, gpost_ref, gnext_ref, o_ref, un_ref,
                           n_k)


def out_proj_residual(y, w, w_layer, h, g_post, layer, g_next, next_layer, *, name):
    t, k = y.shape
    d = w.shape[2]
    tm, tk = OUT_TM, OUT_TK
    n, n_k = t // tm, k // tk
    rows, tile_in, slice_out, step = _lagged_tile_maps(n, n_k, seeded=False)
    finished = pl.BlockSpec((tm // n_k, d), lambda i, j: (slice_out(i, j), 0))
    return pl.pallas_call(
        functools.partial(_out_proj_kernel, n_k=n_k),
        grid=(rows, n_k),
        in_specs=[pl.BlockSpec((tm, tk), lambda i, j: (tile_in(i, j), step(i, j))),
                  pl.BlockSpec((None, k, d), lambda i, j: (w_layer, 0, 0),
                               pipeline_mode=pl.Buffered(1)),
                  finished,
                  _gain_spec(d, layer),
                  _gain_spec(d, next_layer)],
        out_specs=[finished, finished],
        out_shape=[jax.ShapeDtypeStruct((t, d), F32), jax.ShapeDtypeStruct((t, d), BF16)],
        scratch_shapes=[pltpu.VMEM((2, tm, d), F32)],
        compiler_params=_params(("arbitrary", "arbitrary")),
        name=name,
    )(y, w, h, _gains(g_post), _gains(g_next))


def _mlp_product(u, wup, wdn):
    hid = jnp.maximum(jnp.dot(u, wup, preferred_element_type=F32), 0.0)
    return jnp.dot((hid * hid).astype(BF16), wdn, preferred_element_type=F32)


def _mlp_first_tile_kernel(u_ref, wup_ref, wdn_ref, acc_ref, wupb_ref, wdnb_ref):
    wup = wup_ref[...].astype(BF16)
    wdn = wdn_ref[...].astype(BF16)
    wupb_ref[...] = wup
    wdnb_ref[...] = wdn

    @pl.when(pl.program_id(0) == 0)
    def _():
        acc_ref[...] = jnp.zeros_like(acc_ref)

    acc_ref[...] += _mlp_product(u_ref[...], wup, wdn)


def mlp_first_tile(u, wup, wdn, layer, *, name):
    t, d = u.shape
    f = wup.shape[2]
    tm, tf = MLP_TM, MLP_FIRST_TF
    return pl.pallas_call(
        _mlp_first_tile_kernel,
        grid=(f // tf,),
        in_specs=[pl.BlockSpec((tm, d), lambda j: (0, 0)),
                  pl.BlockSpec((None, d, tf), lambda j: (layer, 0, j)),
                  pl.BlockSpec((None, tf, d), lambda j: (layer, j, 0))],
        out_specs=[pl.BlockSpec((tm, d), lambda j: (0, 0)),
                   pl.BlockSpec((d, tf), lambda j: (0, j)),
                   pl.BlockSpec((tf, d), lambda j: (j, 0))],
        out_shape=[jax.ShapeDtypeStruct((tm, d), F32),
                   jax.ShapeDtypeStruct((d, f), BF16),
                   jax.ShapeDtypeStruct((f, d), BF16)],
        compiler_params=_params(("arbitrary",)),
        name=name,
    )(u, wup, wdn)


def _mlp_kernel(u_ref, seed_ref, h_ref, wup_ref, wdn_ref, gpost_ref, *rest, n_f, emit_u):
    if emit_u:
        gnext_ref, o_ref, un_ref, acc_ref = rest
    else:
        (o_ref, acc_ref), gnext_ref, un_ref = rest, None, None

    def accumulate(add):
        add(_mlp_product(u_ref[...], wup_ref[...], wdn_ref[...]))

    _lagged_residual_steps(accumulate, acc_ref, seed_ref, h_ref, gpost_ref, gnext_ref, o_ref,
                           un_ref, n_f)


def mlp_residual(u, h, wup, wdn, gpost, layer, g_next=None, next_layer=None, *, name):
    t, d = h.shape
    f = wup.shape[2]
    tm, tf = MLP_TM, MLP_TF
    n, n_f = t // tm, f // tf
    emit_u = g_next is not None
    seed, wup, wdn = mlp_first_tile(u, wup, wdn, layer, name=name + "_first")
    rows, tile_in, slice_out, step = _lagged_tile_maps(n, n_f, seeded=True)
    finished = pl.BlockSpec((tm // n_f, d), lambda i, j: (slice_out(i, j), 0))
    in_specs = [pl.BlockSpec((tm, d), lambda i, j: (tile_in(i, j), 0)),
                pl.BlockSpec((tm // n_f, d), lambda i, j: (jnp.where(i == 0, j, n_f - 1), 0)),
                finished,
                pl.BlockSpec((d, tf), lambda i, j: (0, step(i, j))),
                pl.BlockSpec((tf, d), lambda i, j: (step(i, j), 0)),
                _gain_spec(d, layer)]
    args = [u, seed, h, wup, wdn, _gains(gpost)]
    out_specs = [finished]
    out_shape = [jax.ShapeDtypeStruct((t, d), F32)]
    if emit_u:
        in_specs.append(_gain_spec(d, next_layer))
        args.append(_gains(g_next))
        out_specs.append(finished)
        out_shape.append(jax.ShapeDtypeStruct((t, d), BF16))
    return pl.pallas_call(
        functools.partial(_mlp_kernel, n_f=n_f, emit_u=emit_u),
        grid=(rows, n_f),
        in_specs=in_specs,
        out_specs=out_specs,
        out_shape=out_shape,
        scratch_shapes=[pltpu.VMEM((2, tm, d), F32)],
        compiler_params=_params(("arbitrary", "arbitrary")),
        name=name,
    )(*args)


def _retention_kernel(tab_ref, q_ref, k_ref, v_ref, g_ref, cos_ref, sin_ref, gnw_ref,
                      o_ref, state_ref, dmask_ref, xi_ref, zeta_ref, *, n_heads, blk):
    head = pl.program_id(2)
    half = RET_HEAD_DK // 2
    log_gamma = tab_ref[head]
    gamma_blk = tab_ref[n_heads + head]

    @pl.when(pl.program_id(1) == 0)
    def _():
        state_ref[head] = jnp.zeros(state_ref.shape[1:], F32)
        ri = lax.broadcasted_iota(jnp.int32, (blk, blk), 0)
        ci = lax.broadcasted_iota(jnp.int32, (blk, blk), 1)
        dist = jnp.abs(ri - ci).astype(F32)
        shift = int(math.log2(CHUNK))
        visible = jnp.right_shift(ci, shift) <= jnp.right_shift(ri, shift)
        dmask_ref[head] = jnp.where(visible, jnp.exp(dist * log_gamma), 0.0)
        pos = lax.broadcasted_iota(jnp.int32, (blk, V7X_LANES), 0).astype(F32)
        xi_ref[head] = jnp.exp((pos + 1.0) * log_gamma)
        zeta_ref[head] = jnp.exp((blk - 1.0 - pos) * log_gamma)

    xi = jnp.concatenate([xi_ref[head]] * (RET_HEAD_DV // V7X_LANES), axis=1)
    zeta = jnp.concatenate([zeta_ref[head]] * (RET_HEAD_DK // V7X_LANES), axis=1)
    dmask = dmask_ref[head]
    gnw = gnw_ref[...]

    for sub in range(q_ref.shape[0] // blk):
        rows = pl.ds(sub * blk, blk)
        cos = cos_ref[rows, :]
        sin = sin_ref[rows, :]

        def rotary(t):
            t1, t2 = t[:, :half], t[:, half:]
            return jnp.concatenate([t1 * cos - t2 * sin, t1 * sin + t2 * cos], axis=1)

        qr = rotary(q_ref[rows, :].astype(F32))
        kr = rotary(k_ref[rows, :].astype(F32)) * (RET_HEAD_DK ** -0.5)
        qb = qr.astype(BF16)
        kb = kr.astype(BF16)
        vb = v_ref[rows, :].astype(BF16)

        scores = lax.dot_general(qb, kb, (((1,), (1,)), ((), ())), preferred_element_type=F32)
        inner = jnp.dot((scores * dmask).astype(BF16), vb, preferred_element_type=F32)
        state = state_ref[head]
        cross = jnp.dot(qb, state.astype(BF16), preferred_element_type=F32)
        o = inner + cross * xi

        kz = (kr * zeta).astype(BF16)
        upd = lax.dot_general(kz, vb, (((0,), (0,)), ((), ())), preferred_element_type=F32)
        state_ref[head] = state * gamma_blk + upd

        mu = jnp.mean(o, axis=-1, keepdims=True)
        dev = o - mu
        var = jnp.mean(dev * dev, axis=-1, keepdims=True)
        normed = dev * lax.rsqrt(var + GN_EPS) * gnw
        o_ref[rows, :] = (_silu(g_ref[rows, :].astype(F32)) * normed).astype(o_ref.dtype)


def retention_core(proj, gn_w, *, batch, seq, name):
    t = proj.shape[0]
    n_heads = proj.shape[1] // (2 * RET_HEAD_DK + 2 * RET_HEAD_DV)
    step, blk = RET_STEP, RET_BLK
    steps = seq // step
    dv_per_dk = RET_HEAD_DV // RET_HEAD_DK
    k_off = n_heads
    v_off = 2 * n_heads // dv_per_dk
    g_off = v_off + n_heads

    half = RET_HEAD_DK // 2
    inv_freq = ROPE_BASE ** (-jnp.arange(half, dtype=F32) / half)
    ang = jnp.arange(seq).astype(F32)[:, None] * inv_freq[None, :]
    cos, sin = jnp.cos(ang), jnp.sin(ang)
    log_gamma = jnp.log1p(-jnp.exp2(-5.0 - jnp.arange(n_heads, dtype=F32)))
    tab = jnp.concatenate([log_gamma, jnp.exp(blk * log_gamma)])

    row = lambda b, c, h: b * steps + c
    return pl.pallas_call(
        functools.partial(_retention_kernel, n_heads=n_heads, blk=blk),
        grid=(batch, steps, n_heads),
        in_specs=[pl.BlockSpec(memory_space=pltpu.SMEM),
                  pl.BlockSpec((step, RET_HEAD_DK), lambda b, c, h: (row(b, c, h), h)),
                  pl.BlockSpec((step, RET_HEAD_DK), lambda b, c, h: (row(b, c, h), k_off + h)),
                  pl.BlockSpec((step, RET_HEAD_DV), lambda b, c, h: (row(b, c, h), v_off + h)),
                  pl.BlockSpec((step, RET_HEAD_DV), lambda b, c, h: (row(b, c, h), g_off + h)),
                  pl.BlockSpec((step, half), lambda b, c, h: (c, 0)),
                  pl.BlockSpec((step, half), lambda b, c, h: (c, 0)),
                  pl.BlockSpec((1, RET_HEAD_DV), lambda b, c, h: (0, h))],
        out_specs=pl.BlockSpec((step, RET_HEAD_DV), lambda b, c, h: (row(b, c, h), h)),
        out_shape=jax.ShapeDtypeStruct((t, n_heads * RET_HEAD_DV), BF16),
        scratch_shapes=[pltpu.VMEM((n_heads, RET_HEAD_DK, RET_HEAD_DV), F32),
                        pltpu.VMEM((n_heads, blk, blk), F32),
                        pltpu.VMEM((n_heads, blk, V7X_LANES), F32),
                        pltpu.VMEM((n_heads, blk, V7X_LANES), F32)],
        compiler_params=_params(("parallel", "arbitrary", "arbitrary")),
        name=name,
    )(tab, proj, proj, proj, proj, cos, sin, gn_w.reshape(1, -1))


def _softplus(x):
    return jnp.maximum(x, 0.0) + jnp.log1p(jnp.exp(-jnp.abs(x)))


def _split_bf16(v, terms):
    parts = []
    for _ in range(terms - 1):
        part = v.astype(BF16)
        parts.append(part)
        v = v - part.astype(F32)
    parts.append(v.astype(BF16))
    return jnp.concatenate(parts, axis=0)


def _sum_terms(stacked, rows):
    out = stacked[0:rows]
    for i in range(1, stacked.shape[0] // rows):
        out = out + stacked[i * rows:(i + 1) * rows]
    return out


def _ssd_decay_kernel(dt_ref, prm_ref, src_ref, acum_ref, inw_ref, *, blk):
    prm = prm_ref[...]
    a2 = -jnp.exp(prm[:, 1:2]) * LOG2_E
    ri = lax.broadcasted_iota(jnp.int32, (blk, blk), 0)
    ci = lax.broadcasted_iota(jnp.int32, (blk, blk), 1)
    upper_ones = jnp.where(ri <= ci, 1.0, 0.0).astype(BF16)
    n_heads = dt_ref.shape[0]
    for sub in range(dt_ref.shape[1] // blk):
        cols = pl.ds(sub * blk, blk)
        dt = _softplus(dt_ref[:, cols] + prm[:, 0:1])
        acum = _sum_terms(jnp.dot(_split_bf16(dt * a2, 3), upper_ones,
                                  preferred_element_type=F32), n_heads)
        src_ref[:, cols] = acum - jnp.log2(dt)
        acum_ref[:, cols] = acum
        inw_ref[:, cols] = dt * jnp.exp2(acum[:, blk - 1:blk] - acum)


def ssd_decays(dt_raw_t, dt_bias, a_log, *, blk, name):
    n_heads, t = dt_raw_t.shape
    step = SSD_DECAY_STEP
    spec = pl.BlockSpec((n_heads, step), lambda i: (0, i))
    shape = jax.ShapeDtypeStruct((n_heads, t), F32)
    return pl.pallas_call(
        functools.partial(_ssd_decay_kernel, blk=blk),
        grid=(t // step,),
        in_specs=[spec, pl.BlockSpec((n_heads, 2), lambda i: (0, 0))],
        out_specs=[spec, spec, spec],
        out_shape=[shape, shape, shape],
        compiler_params=_params(("parallel",)),
        name=name,
    )(dt_raw_t, jnp.stack([dt_bias, a_log], axis=-1))


def _ssd_kernel(z_ref, x_ref, b_ref, c_ref, wx_ref, wb_ref, wc_ref, bx_ref, bb_ref, bc_ref,
                src_ref, acum_ref, inw_ref, dexp_ref, nw_ref, o_ref,
                state_ref, xe_ref, be_ref, ce_ref, *, blk):
    step_rows = x_ref.shape[0]
    halo = V7X_SUBLANES

    @pl.when(pl.program_id(2) == 0)
    def _():
        state_ref[...] = jnp.zeros_like(state_ref)
        xe_ref[0:halo, :] = jnp.zeros((halo, xe_ref.shape[1]), F32)
        be_ref[0:halo, :] = jnp.zeros((halo, be_ref.shape[1]), F32)
        ce_ref[0:halo, :] = jnp.zeros((halo, ce_ref.shape[1]), F32)

    xe_ref[halo:, :] = x_ref[...].astype(F32)
    be_ref[halo:, :] = b_ref[...].astype(F32)
    ce_ref[halo:, :] = c_ref[...].astype(F32)

    def conv_silu(ext_ref, w_ref, bias_ref, off):
        out = bias_ref[...] + ext_ref[pl.ds(halo + off, blk), :] * w_ref[SSD_CONV_W - 1:SSD_CONV_W, :]
        for s in range(1, SSD_CONV_W):
            out = out + (ext_ref[pl.ds(halo + off - s, blk), :]
                         * w_ref[SSD_CONV_W - 1 - s:SSD_CONV_W - s, :])
        return _silu(out)

    ri = lax.broadcasted_iota(jnp.int32, (blk, blk), 0)
    ci = lax.broadcasted_iota(jnp.int32, (blk, blk), 1)
    causal = ri >= ci
    lane = lax.broadcasted_iota(jnp.int32, (blk, V7X_LANES), 1)
    first = lane < SSD_HEADDIM
    keep_first = jnp.where(first, 1.0, 0.0).astype(BF16)
    keep_second = jnp.where(first, 0.0, 1.0).astype(BF16)
    dexp = dexp_ref[...]
    nw = nw_ref[...]
    n_heads = acum_ref.shape[0]
    gw = x_ref.shape[1]

    def head_selector(terms, lanes_per_head):
        k = lax.broadcasted_iota(jnp.int32, (terms * n_heads, n_heads * lanes_per_head), 0)
        n = lax.broadcasted_iota(jnp.int32, (terms * n_heads, n_heads * lanes_per_head), 1)
        shift = int(math.log2(lanes_per_head))
        return jnp.where((k & (n_heads - 1)) == jnp.right_shift(n, shift), 1.0, 0.0).astype(BF16)

    sel_mask = head_selector(3, blk)
    sel_chan = head_selector(2, SSD_HEADDIM)
    transposed_lhs = (((0,), (0,)), ((), ()))

    for sub in range(step_rows // blk):
        off = sub * blk
        src_r = src_ref[:, pl.ds(off, blk)]
        acum_r = acum_ref[:, pl.ds(off, blk)]
        acum_bc = lax.dot_general(_split_bf16(acum_r, 3), sel_mask, transposed_lhs,
                                  preferred_element_type=F32)
        in_weight = lax.dot_general(_split_bf16(inw_ref[:, pl.ds(off, blk)], 2), sel_chan,
                                    transposed_lhs, preferred_element_type=F32)
        xs = conv_silu(xe_ref, wx_ref, bx_ref, off)
        bmb = conv_silu(be_ref, wb_ref, bb_ref, off).astype(BF16)
        cmb = conv_silu(ce_ref, wc_ref, bc_ref, off).astype(BF16)

        cb = lax.dot_general(cmb, bmb, (((1,), (1,)), ((), ())), preferred_element_type=F32)
        state = state_ref[...]
        y_off = jnp.dot(cmb, state.astype(BF16), preferred_element_type=F32)

        y_tiles, decay_tiles = [], []
        for pair in range(gw // V7X_LANES):
            r0, r1 = 2 * pair, 2 * pair + 1
            cols = slice(pair * V7X_LANES, (pair + 1) * V7X_LANES)
            xp = xs[:, cols]
            xpb = xp.astype(BF16)
            out_decay = jnp.exp2(jnp.where(first, acum_bc[:, r0 * blk:r0 * blk + V7X_LANES],
                                           acum_bc[:, r1 * blk:r1 * blk + V7X_LANES]))
            y = y_off[:, cols] * out_decay + dexp[:, cols] * xp
            for r, keep in ((r0, keep_first), (r1, keep_second)):
                seg = acum_bc[:, r * blk:(r + 1) * blk] - src_r[r:r + 1, :]
                lmat = jnp.exp2(jnp.where(causal, seg, -jnp.inf))
                y = y + jnp.dot((cb * lmat).astype(BF16), xpb * keep,
                                preferred_element_type=F32)
            y_tiles.append(y)
            decay_tiles.append(out_decay[blk - 1:blk, :])
        y = jnp.concatenate(y_tiles, axis=1)
        xw = (xs * in_weight).astype(BF16)
        upd = lax.dot_general(bmb, xw, transposed_lhs, preferred_element_type=F32)
        state_ref[...] = state * jnp.concatenate(decay_tiles, axis=1) + upd

        yg = y * _silu(z_ref[pl.ds(off, blk), :].astype(F32))
        o_ref[pl.ds(off, blk), :] = (yg * _rms_scale(yg) * nw).astype(o_ref.dtype)

    xe_ref[0:halo, :] = xe_ref[step_rows:step_rows + halo, :]
    be_ref[0:halo, :] = be_ref[step_rows:step_rows + halo, :]
    ce_ref[0:halo, :] = ce_ref[step_rows:step_rows + halo, :]


def ssd_core(zxbc, dt_raw_t, conv_w, conv_b, dt_bias, a_log, d_skip, norm_w, *, batch, seq, name):
    t = zxbc.shape[0]
    g, r, n, gw = SSD_GROUPS, SSD_HEADS_PER_GROUP, SSD_STATE, SSD_GROUP_WIDTH
    d_inner = g * gw
    step, blk = SSD_STEP, SSD_BLK
    steps = seq // step
    x_off = d_inner // gw
    b_off = 2 * d_inner // n
    c_off = b_off + g
    wb_off = d_inner // n
    wc_off = wb_off + g

    src_r, acum_r, inw_r = (a.reshape(g, r, t) for a in
                           ssd_decays(dt_raw_t, dt_bias, a_log, blk=blk, name=name + "_decays"))
    dexp = jnp.repeat(d_skip, SSD_HEADDIM).reshape(1, d_inner)
    conv_b = conv_b.reshape(1, -1)

    row = lambda b, gi, c: b * steps + c
    return pl.pallas_call(
        functools.partial(_ssd_kernel, blk=blk),
        grid=(batch, g, steps),
        in_specs=[pl.BlockSpec((step, gw), lambda b, gi, c: (row(b, gi, c), gi)),
                  pl.BlockSpec((step, gw), lambda b, gi, c: (row(b, gi, c), x_off + gi)),
                  pl.BlockSpec((step, n), lambda b, gi, c: (row(b, gi, c), b_off + gi)),
                  pl.BlockSpec((step, n), lambda b, gi, c: (row(b, gi, c), c_off + gi)),
                  pl.BlockSpec((SSD_CONV_W, gw), lambda b, gi, c: (0, gi)),
                  pl.BlockSpec((SSD_CONV_W, n), lambda b, gi, c: (0, wb_off + gi)),
                  pl.BlockSpec((SSD_CONV_W, n), lambda b, gi, c: (0, wc_off + gi)),
                  pl.BlockSpec((1, gw), lambda b, gi, c: (0, gi)),
                  pl.BlockSpec((1, n), lambda b, gi, c: (0, wb_off + gi)),
                  pl.BlockSpec((1, n), lambda b, gi, c: (0, wc_off + gi)),
                  pl.BlockSpec((None, r, step), lambda b, gi, c: (gi, 0, row(b, gi, c))),
                  pl.BlockSpec((None, r, step), lambda b, gi, c: (gi, 0, row(b, gi, c))),
                  pl.BlockSpec((None, r, step), lambda b, gi, c: (gi, 0, row(b, gi, c))),
                  pl.BlockSpec((1, gw), lambda b, gi, c: (0, gi)),
                  pl.BlockSpec((1, gw), lambda b, gi, c: (0, gi))],
        out_specs=pl.BlockSpec((step, gw), lambda b, gi, c: (row(b, gi, c), gi)),
        out_shape=jax.ShapeDtypeStruct((t, d_inner), BF16),
        scratch_shapes=[pltpu.VMEM((n, gw), F32),
                        pltpu.VMEM((step + V7X_SUBLANES, gw), F32),
                        pltpu.VMEM((step + V7X_SUBLANES, n), F32),
                        pltpu.VMEM((step + V7X_SUBLANES, n), F32)],
        compiler_params=_params(("parallel", "parallel", "arbitrary")),
        name=name,
    )(zxbc, zxbc, zxbc, zxbc, conv_w, conv_w, conv_w, conv_b, conv_b, conv_b,
      src_r, acum_r, inw_r, dexp, norm_w.reshape(1, d_inner))


def kernel(x, norm_mix_pre, norm_mix_post, norm_ffn_pre, norm_ffn_post, ret_w_in, ret_gn_w, ret_w_out, ssd_w_in, ssd_conv_w, ssd_conv_b, ssd_dt_bias, ssd_a_log, ssd_d, ssd_norm_w, ssd_w_out, mlp_w_up, mlp_w_down):
    batch, seq, d = x.shape
    h = x.reshape(batch * seq, d)

    n_zxbc = ssd_w_in.shape[2] - SSD_GROUPS * SSD_HEADS_PER_GROUP
    ssd_w_in_t = jnp.swapaxes(ssd_w_in, 1, 2)
    ret_w_in, ret_w_out, ssd_w_out = (w.astype(BF16) for w in (ret_w_in, ret_w_out, ssd_w_out))

    proj = norm_matmul(h, norm_mix_pre, 0, ret_w_in, 0, out_dtype=BF16, name="ret_in_proj")
    y = retention_core(proj, ret_gn_w[0], batch=batch, seq=seq, name="retention_core")
    h, u = out_proj_residual(y, ret_w_out, 0, h, norm_mix_post, 0, norm_ffn_pre, 0,
                             name="ret_out_proj")
    h, u = mlp_residual(u, h, mlp_w_up, mlp_w_down, norm_ffn_post, 0, norm_mix_pre, 1, name="mlp0")

    zxbc, dt_raw_t = matmul_dt(u, ssd_w_in_t, 0, n_out=n_zxbc, out_dtype=F32,
                               name="ssd_in_proj")
    y = ssd_core(zxbc, dt_raw_t, ssd_conv_w[0], ssd_conv_b[0], ssd_dt_bias[0], ssd_a_log[0],
                 ssd_d[0], ssd_norm_w[0], batch=batch, seq=seq, name="ssd_core")
    h, u = out_proj_residual(y, ssd_w_out, 0, h, norm_mix_post, 1, norm_ffn_pre, 1,
                             name="ssd_out_proj")
    (h,) = mlp_residual(u, h, mlp_w_up, mlp_w_down, norm_ffn_post, 1, name="mlp1")
    return h.reshape(batch, seq, d)
```

```python
import functools
import math

import jax
import jax.numpy as jnp
from jax import lax
from jax.experimental import pallas as pl
from jax.experimental.pallas import tpu as pltpu

F32 = jnp.float32
BF16 = jnp.bfloat16

RMS_EPS = 1e-6
GN_EPS = 1e-5
ROPE_BASE = 10000.0
CHUNK = 64
LOG2_E = math.log2(math.e)

RET_HEAD_DK = 256
RET_HEAD_DV = 512
SSD_HEADDIM = 64
SSD_GROUPS = 8
SSD_HEADS_PER_GROUP = 8
SSD_STATE = 128
SSD_CONV_W = 4
SSD_GROUP_WIDTH = SSD_HEADS_PER_GROUP * SSD_HEADDIM

V7X_LANES = 128
V7X_SUBLANES = 8
V7X_VMEM_LIMIT_BYTES = 56 * 1024 * 1024

PROJ_TM, PROJ_TN = 1024, 2048
PROJ_WCAST_TN = 1024
OUT_TM, OUT_TK = 512, 2048
MLP_TM, MLP_TF = 1024, 1024
MLP_FIRST_TF = 512
RET_STEP, RET_BLK = 2048, 256
SSD_DECAY_STEP = 2048
SSD_STEP, SSD_BLK = 2048, 128


def _params(semantics):
    return pltpu.CompilerParams(dimension_semantics=semantics,
                                vmem_limit_bytes=V7X_VMEM_LIMIT_BYTES)


def _rms_scale(x):
    return lax.rsqrt(jnp.mean(x * x, axis=-1, keepdims=True) + RMS_EPS)


def _silu(x):
    half = 0.5 * x
    return half + half * jnp.tanh(half)


def _norm_matmul_kernel(x_ref, g_ref, w_ref, o_ref, u_ref):
    @pl.when(pl.program_id(1) == 0)
    def _():
        x = x_ref[...]
        u_ref[...] = (x * _rms_scale(x) * g_ref[...]).astype(BF16)

    o_ref[...] = jnp.dot(u_ref[...], w_ref[...],
                         preferred_element_type=F32).astype(o_ref.dtype)


def _matmul_dt_kernel(u_ref, wt_ref, wdt_ref, o_ref, dt_ref, wb_ref):
    contract_last = (((1,), (1,)), ((), ()))

    @pl.when(pl.program_id(1) == 0)
    def _():
        wb_ref[...] = wt_ref[...].astype(BF16)

    @pl.when(pl.program_id(0) == 0)
    def _():
        dt_ref[...] = lax.dot_general(wdt_ref[...].astype(BF16), u_ref[...], contract_last,
                                      preferred_element_type=F32)

    o_ref[...] = lax.dot_general(u_ref[...], wb_ref[...], contract_last,
                                 preferred_element_type=F32).astype(o_ref.dtype)


def _gain_spec(d, layer):
    return pl.BlockSpec((None, 1, d), lambda *_: (layer, 0, 0))


def _gains(g):
    return g.reshape(g.shape[0], 1, g.shape[1])


def norm_matmul(h, g, layer, w, w_layer, *, out_dtype, name):
    t, d = h.shape
    n = w.shape[2]
    tm, tn = PROJ_TM, PROJ_TN
    return pl.pallas_call(
        _norm_matmul_kernel,
        grid=(t // tm, n // tn),
        in_specs=[pl.BlockSpec((tm, d), lambda i, j: (i, 0)),
                  _gain_spec(d, layer),
                  pl.BlockSpec((None, d, tn), lambda i, j: (w_layer, 0, j))],
        out_specs=pl.BlockSpec((tm, tn), lambda i, j: (i, j)),
        out_shape=jax.ShapeDtypeStruct((t, n), out_dtype),
        scratch_shapes=[pltpu.VMEM((tm, d), BF16)],
        compiler_params=_params(("parallel", "arbitrary")),
        name=name,
    )(h, _gains(g), w)


def matmul_dt(u, w_t, w_layer, *, n_out, out_dtype, name):
    t, d = u.shape
    n_dt = w_t.shape[1] - n_out
    tm, tn = PROJ_TM, PROJ_WCAST_TN
    n_rows = t // tm
    dt_rows = n_out // n_dt
    dt_block = lambda j, i: (0, jnp.where(j == 0, i, n_rows - 1))
    return pl.pallas_call(
        _matmul_dt_kernel,
        grid=(n_out // tn, n_rows),
        in_specs=[pl.BlockSpec((tm, d), lambda j, i: (i, 0)),
                  pl.BlockSpec((None, tn, d), lambda j, i: (w_layer, j, 0)),
                  pl.BlockSpec((None, n_dt, d), lambda j, i: (w_layer, dt_rows, 0))],
        out_specs=[pl.BlockSpec((tm, tn), lambda j, i: (i, j)),
                   pl.BlockSpec((n_dt, tm), dt_block)],
        out_shape=[jax.ShapeDtypeStruct((t, n_out), out_dtype),
                   jax.ShapeDtypeStruct((n_dt, t), F32)],
        scratch_shapes=[pltpu.VMEM((tn, d), BF16)],
        compiler_params=_params(("arbitrary", "arbitrary")),
        name=name,
    )(u, w_t, w_t)


def _residual_norm(h, m, g_post, g_next=None):
    hn = h + m * _rms_scale(m) * g_post
    if g_next is None:
        return hn, None
    return hn, (hn * _rms_scale(hn) * g_next).astype(BF16)


def _lagged_residual_steps(accumulate, acc_ref, seed_ref, h_ref, gpost_ref, gnext_ref, o_ref,
                           un_ref, n_slices):
    i = pl.program_id(0)
    j = pl.program_id(1)
    last = pl.num_programs(0) - 1
    first = 0 if seed_ref is None else 1
    slice_rows = acc_ref.shape[1] // n_slices

    def add(part):
        slot = (i + first) % 2
        acc_ref[slot] = jnp.where(j == 0, part, acc_ref[slot] + part)

    def finish_slice(m):
        hn, un = _residual_norm(h_ref[...], m, gpost_ref[...],
                                None if gnext_ref is None else gnext_ref[...])
        o_ref[...] = hn
        if un_ref is not None:
            un_ref[...] = un

    def finish_from_acc():
        rows = pl.ds(pl.multiple_of(j * slice_rows, slice_rows), slice_rows)
        finish_slice(acc_ref[(i + first + 1) % 2, rows, :])

    @pl.when((i == 0) & (j == 0))
    def _():
        acc_ref[...] = jnp.zeros_like(acc_ref)

    @pl.when(i == 0)
    def _():
        if seed_ref is not None:
            finish_slice(seed_ref[...])
        accumulate(add)

    @pl.when((i > 0) & (i < last))
    def _():
        finish_from_acc()
        accumulate(add)

    @pl.when(i == last)
    def _():
        finish_from_acc()


def _lagged_tile_maps(n_tiles, n_steps, seeded):
    first = 1 if seeded else 0
    rows = n_tiles - first + 1
    tile_in = lambda i, j: jnp.minimum(i + first, n_tiles - 1)
    slice_out = lambda i, j: jnp.where(i + first == 0, 0, (i + first - 1) * n_steps + j)
    step = lambda i, j: jnp.where(i < rows - 1, j, n_steps - 1)
    return rows, tile_in, slice_out, step


def _out_proj_kernel(y_ref, w_ref, h_ref, gpost_ref, gnext_ref, o_ref, un_ref, acc_ref, *, n_k):
    tk = y_ref.shape[1]

    def accumulate(add):
        w_rows = pl.ds(pl.multiple_of(pl.program_id(1) * tk, tk), tk)
        add(jnp.dot(y_ref[...], w_ref[w_rows, :], preferred_element_type=F32))

    _lagged_residual_steps(accumulate, acc_ref, None, h_ref, gpost_ref, gnext_ref, o_ref, un_ref,
                           n_k)


def out_proj_residual(y, w, w_layer, h, g_post, layer, g_next, next_layer, *, name):
    t, k = y.shape
    d = w.shape[2]
    tm, tk = OUT_TM, OUT_TK
    n, n_k = t // tm, k // tk
    rows, tile_in, slice_out, step = _lagged_tile_maps(n, n_k, seeded=False)
    finished = pl.BlockSpec((tm // n_k, d), lambda i, j: (slice_out(i, j), 0))
    return pl.pallas_call(
        functools.partial(_out_proj_kernel, n_k=n_k),
        grid=(rows, n_k),
        in_specs=[pl.BlockSpec((tm, tk), lambda i, j: (tile_in(i, j), step(i, j))),
                  pl.BlockSpec((None, k, d), lambda i, j: (w_layer, 0, 0),
                               pipeline_mode=pl.Buffered(1)),
                  finished,
                  _gain_spec(d, layer),
                  _gain_spec(d, next_layer)],
        out_specs=[finished, finished],
        out_shape=[jax.ShapeDtypeStruct((t, d), F32), jax.ShapeDtypeStruct((t, d), BF16)],
        scratch_shapes=[pltpu.VMEM((2, tm, d), F32)],
        compiler_params=_params(("arbitrary", "arbitrary")),
        name=name,
    )(y, w, h, _gains(g_post), _gains(g_next))


def _mlp_product(u, wup, wdn):
    hid = jnp.maximum(jnp.dot(u, wup, preferred_element_type=F32), 0.0)
    return jnp.dot((hid * hid).astype(BF16), wdn, preferred_element_type=F32)


def _mlp_first_tile_kernel(u_ref, wup_ref, wdn_ref, acc_ref, wupb_ref, wdnb_ref):
    wup = wup_ref[...].astype(BF16)
    wdn = wdn_ref[...].astype(BF16)
    wupb_ref[...] = wup
    wdnb_ref[...] = wdn

    @pl.when(pl.program_id(0) == 0)
    def _():
        acc_ref[...] = jnp.zeros_like(acc_ref)

    acc_ref[...] += _mlp_product(u_ref[...], wup, wdn)


def mlp_first_tile(u, wup, wdn, layer, *, name):
    t, d = u.shape
    f = wup.shape[2]
    tm, tf = MLP_TM, MLP_FIRST_TF
    return pl.pallas_call(
        _mlp_first_tile_kernel,
        grid=(f // tf,),
        in_specs=[pl.BlockSpec((tm, d), lambda j: (0, 0)),
                  pl.BlockSpec((None, d, tf), lambda j: (layer, 0, j)),
                  pl.BlockSpec((None, tf, d), lambda j: (layer, j, 0))],
        out_specs=[pl.BlockSpec((tm, d), lambda j: (0, 0)),
                   pl.BlockSpec((d, tf), lambda j: (0, j)),
                   pl.BlockSpec((tf, d), lambda j: (j, 0))],
        out_shape=[jax.ShapeDtypeStruct((tm, d), F32),
                   jax.ShapeDtypeStruct((d, f), BF16),
                   jax.ShapeDtypeStruct((f, d), BF16)],
        compiler_params=_params(("arbitrary",)),
        name=name,
    )(u, wup, wdn)


def _mlp_kernel(u_ref, seed_ref, h_ref, wup_ref, wdn_ref, gpost_ref, *rest, n_f, emit_u):
    if emit_u:
        gnext_ref, o_ref, un_ref, acc_ref = rest
    else:
        (o_ref, acc_ref), gnext_ref, un_ref = rest, None, None

    def accumulate(add):
        add(_mlp_product(u_ref[...], wup_ref[...], wdn_ref[...]))

    _lagged_residual_steps(accumulate, acc_ref, seed_ref, h_ref, gpost_ref, gnext_ref, o_ref,
                           un_ref, n_f)


def mlp_residual(u, h, wup, wdn, gpost, layer, g_next=None, next_layer=None, *, name):
    t, d = h.shape
    f = wup.shape[2]
    tm, tf = MLP_TM, MLP_TF
    n, n_f = t // tm, f // tf
    emit_u = g_next is not None
    seed, wup, wdn = mlp_first_tile(u, wup, wdn, layer, name=name + "_first")
    rows, tile_in, slice_out, step = _lagged_tile_maps(n, n_f, seeded=True)
    finished = pl.BlockSpec((tm // n_f, d), lambda i, j: (slice_out(i, j), 0))
    in_specs = [pl.BlockSpec((tm, d), lambda i, j: (tile_in(i, j), 0)),
                pl.BlockSpec((tm // n_f, d), lambda i, j: (jnp.where(i == 0, j, n_f - 1), 0)),
                finished,
                pl.BlockSpec((d, tf), lambda i, j: (0, step(i, j))),
                pl.BlockSpec((tf, d), lambda i, j: (step(i, j), 0)),
                _gain_spec(d, layer)]
    args = [u, seed, h, wup, wdn, _gains(gpost)]
    out_specs = [finished]
    out_shape = [jax.ShapeDtypeStruct((t, d), F32)]
    if emit_u:
        in_specs.append(_gain_spec(d, next_layer))
        args.append(_gains(g_next))
        out_specs.append(finished)
        out_shape.append(jax.ShapeDtypeStruct((t, d), BF16))
    return pl.pallas_call(
        functools.partial(_mlp_kernel, n_f=n_f, emit_u=emit_u),
        grid=(rows, n_f),
        in_specs=in_specs,
        out_specs=out_specs,
        out_shape=out_shape,
        scratch_shapes=[pltpu.VMEM((2, tm, d), F32)],
        compiler_params=_params(("arbitrary", "arbitrary")),
        name=name,
    )(*args)


def _retention_kernel(tab_ref, q_ref, k_ref, v_ref, g_ref, cos_ref, sin_ref, gnw_ref,
                      o_ref, state_ref, dmask_ref, xi_ref, zeta_ref, *, n_heads, blk):
    head = pl.program_id(2)
    half = RET_HEAD_DK // 2
    log_gamma = tab_ref[head]
    gamma_blk = tab_ref[n_heads + head]

    @pl.when(pl.program_id(1) == 0)
    def _():
        state_ref[head] = jnp.zeros(state_ref.shape[1:], F32)
        ri = lax.broadcasted_iota(jnp.int32, (blk, blk), 0)
        ci = lax.broadcasted_iota(jnp.int32, (blk, blk), 1)
        dist = jnp.abs(ri - ci).astype(F32)
        shift = int(math.log2(CHUNK))
        visible = jnp.right_shift(ci, shift) <= jnp.right_shift(ri, shift)
        dmask_ref[head] = jnp.where(visible, jnp.exp(dist * log_gamma), 0.0)
        pos = lax.broadcasted_iota(jnp.int32, (blk, V7X_LANES), 0).astype(F32)
        xi_ref[head] = jnp.exp((pos + 1.0) * log_gamma)
        zeta_ref[head] = jnp.exp((blk - 1.0 - pos) * log_gamma)

    xi = jnp.concatenate([xi_ref[head]] * (RET_HEAD_DV // V7X_LANES), axis=1)
    zeta = jnp.concatenate([zeta_ref[head]] * (RET_HEAD_DK // V7X_LANES), axis=1)
    dmask = dmask_ref[head]
    gnw = gnw_ref[...]

    for sub in range(q_ref.shape[0] // blk):
        rows = pl.ds(sub * blk, blk)
        cos = cos_ref[rows, :]
        sin = sin_ref[rows, :]

        def rotary(t):
            t1, t2 = t[:, :half], t[:, half:]
            return jnp.concatenate([t1 * cos - t2 * sin, t1 * sin + t2 * cos], axis=1)

        qr = rotary(q_ref[rows, :].astype(F32))
        kr = rotary(k_ref[rows, :].astype(F32)) * (RET_HEAD_DK ** -0.5)
        qb = qr.astype(BF16)
        kb = kr.astype(BF16)
        vb = v_ref[rows, :].astype(BF16)

        scores = lax.dot_general(qb, kb, (((1,), (1,)), ((), ())), preferred_element_type=F32)
        inner = jnp.dot((scores * dmask).astype(BF16), vb, preferred_element_type=F32)
        state = state_ref[head]
        cross = jnp.dot(qb, state.astype(BF16), preferred_element_type=F32)
        o = inner + cross * xi

        kz = (kr * zeta).astype(BF16)
        upd = lax.dot_general(kz, vb, (((0,), (0,)), ((), ())), preferred_element_type=F32)
        state_ref[head] = state * gamma_blk + upd

        mu = jnp.mean(o, axis=-1, keepdims=True)
        dev = o - mu
        var = jnp.mean(dev * dev, axis=-1, keepdims=True)
        normed = dev * lax.rsqrt(var + GN_EPS) * gnw
        o_ref[rows, :] = (_silu(g_ref[rows, :].astype(F32)) * normed).astype(o_ref.dtype)


def _rotary_tables(seq, half, blk):
    inv_freq = ROPE_BASE ** (-jnp.arange(half, dtype=F32) / half)
    ang_blk = (jnp.arange(seq // blk) * blk).astype(F32)[:, None] * inv_freq[None, :]
    ang_in = jnp.arange(blk).astype(F32)[:, None] * inv_freq[None, :]
    cb, sb = jnp.cos(ang_blk)[:, None, :], jnp.sin(ang_blk)[:, None, :]
    ci, si = jnp.cos(ang_in)[None], jnp.sin(ang_in)[None]
    return (cb * ci - sb * si).reshape(seq, half), (sb * ci + cb * si).reshape(seq, half)


def retention_core(proj, gn_w, *, batch, seq, name):
    t = proj.shape[0]
    n_heads = proj.shape[1] // (2 * RET_HEAD_DK + 2 * RET_HEAD_DV)
    step, blk = RET_STEP, RET_BLK
    steps = seq // step
    dv_per_dk = RET_HEAD_DV // RET_HEAD_DK
    k_off = n_heads
    v_off = 2 * n_heads // dv_per_dk
    g_off = v_off + n_heads

    half = RET_HEAD_DK // 2
    cos, sin = _rotary_tables(seq, half, blk)
    log_gamma = jnp.log1p(-jnp.exp2(-5.0 - jnp.arange(n_heads, dtype=F32)))
    tab = jnp.concatenate([log_gamma, jnp.exp(blk * log_gamma)])

    row = lambda b, c, h: b * steps + c
    return pl.pallas_call(
        functools.partial(_retention_kernel, n_heads=n_heads, blk=blk),
        grid=(batch, steps, n_heads),
        in_specs=[pl.BlockSpec(memory_space=pltpu.SMEM),
                  pl.BlockSpec((step, RET_HEAD_DK), lambda b, c, h: (row(b, c, h), h)),
                  pl.BlockSpec((step, RET_HEAD_DK), lambda b, c, h: (row(b, c, h), k_off + h)),
                  pl.BlockSpec((step, RET_HEAD_DV), lambda b, c, h: (row(b, c, h), v_off + h)),
                  pl.BlockSpec((step, RET_HEAD_DV), lambda b, c, h: (row(b, c, h), g_off + h)),
                  pl.BlockSpec((step, half), lambda b, c, h: (c, 0)),
                  pl.BlockSpec((step, half), lambda b, c, h: (c, 0)),
                  pl.BlockSpec((1, RET_HEAD_DV), lambda b, c, h: (0, h))],
        out_specs=pl.BlockSpec((step, RET_HEAD_DV), lambda b, c, h: (row(b, c, h), h)),
        out_shape=jax.ShapeDtypeStruct((t, n_heads * RET_HEAD_DV), BF16),
        scratch_shapes=[pltpu.VMEM((n_heads, RET_HEAD_DK, RET_HEAD_DV), F32),
                        pltpu.VMEM((n_heads, blk, blk), F32),
                        pltpu.VMEM((n_heads, blk, V7X_LANES), F32),
                        pltpu.VMEM((n_heads, blk, V7X_LANES), F32)],
        compiler_params=_params(("parallel", "arbitrary", "arbitrary")),
        name=name,
    )(tab, proj, proj, proj, proj, cos, sin, gn_w.reshape(1, -1))


def _softplus(x):
    return jnp.maximum(x, 0.0) + jnp.log1p(jnp.exp(-jnp.abs(x)))


def _split_bf16(v, terms):
    parts = []
    for _ in range(terms - 1):
        part = v.astype(BF16)
        parts.append(part)
        v = v - part.astype(F32)
    parts.append(v.astype(BF16))
    return jnp.concatenate(parts, axis=0)


def _sum_terms(stacked, rows):
    out = stacked[0:rows]
    for i in range(1, stacked.shape[0] // rows):
        out = out + stacked[i * rows:(i + 1) * rows]
    return out


def _ssd_decay_kernel(dt_ref, prm_ref, src_ref, acum_ref, inw_ref, *, blk):
    prm = prm_ref[...]
    a2 = -jnp.exp(prm[:, 1:2]) * LOG2_E
    ri = lax.broadcasted_iota(jnp.int32, (blk, blk), 0)
    ci = lax.broadcasted_iota(jnp.int32, (blk, blk), 1)
    upper_ones = jnp.where(ri <= ci, 1.0, 0.0).astype(BF16)
    n_heads = dt_ref.shape[0]
    for sub in range(dt_ref.shape[1] // blk):
        cols = pl.ds(sub * blk, blk)
        dt = _softplus(dt_ref[:, cols] + prm[:, 0:1])
        acum = _sum_terms(jnp.dot(_split_bf16(dt * a2, 3), upper_ones,
                                  preferred_element_type=F32), n_heads)
        src_ref[:, cols] = acum - jnp.log2(dt)
        acum_ref[:, cols] = acum
        inw_ref[:, cols] = dt * jnp.exp2(acum[:, blk - 1:blk] - acum)


def ssd_decays(dt_raw_t, dt_bias, a_log, *, blk, name):
    n_heads, t = dt_raw_t.shape
    step = SSD_DECAY_STEP
    spec = pl.BlockSpec((n_heads, step), lambda i: (0, i))
    shape = jax.ShapeDtypeStruct((n_heads, t), F32)
    return pl.pallas_call(
        functools.partial(_ssd_decay_kernel, blk=blk),
        grid=(t // step,),
        in_specs=[spec, pl.BlockSpec((n_heads, 2), lambda i: (0, 0))],
        out_specs=[spec, spec, spec],
        out_shape=[shape, shape, shape],
        compiler_params=_params(("parallel",)),
        name=name,
    )(dt_raw_t, jnp.stack([dt_bias, a_log], axis=-1))


def _ssd_kernel(z_ref, x_ref, b_ref, c_ref, wx_ref, wb_ref, wc_ref, bx_ref, bb_ref, bc_ref,
                src_ref, acum_ref, inw_ref, dexp_ref, nw_ref, o_ref,
                state_ref, xe_ref, be_ref, ce_ref, *, blk):
    step_rows = x_ref.shape[0]
    halo = V7X_SUBLANES

    @pl.when(pl.program_id(2) == 0)
    def _():
        state_ref[...] = jnp.zeros_like(state_ref)
        xe_ref[0:halo, :] = jnp.zeros((halo, xe_ref.shape[1]), F32)
        be_ref[0:halo, :] = jnp.zeros((halo, be_ref.shape[1]), F32)
        ce_ref[0:halo, :] = jnp.zeros((halo, ce_ref.shape[1]), F32)

    xe_ref[halo:, :] = x_ref[...].astype(F32)
    be_ref[halo:, :] = b_ref[...].astype(F32)
    ce_ref[halo:, :] = c_ref[...].astype(F32)

    def conv_silu(ext_ref, w_ref, bias_ref, off):
        out = bias_ref[...] + ext_ref[pl.ds(halo + off, blk), :] * w_ref[SSD_CONV_W - 1:SSD_CONV_W, :]
        for s in range(1, SSD_CONV_W):
            out = out + (ext_ref[pl.ds(halo + off - s, blk), :]
                         * w_ref[SSD_CONV_W - 1 - s:SSD_CONV_W - s, :])
        return _silu(out)

    ri = lax.broadcasted_iota(jnp.int32, (blk, blk), 0)
    ci = lax.broadcasted_iota(jnp.int32, (blk, blk), 1)
    causal = ri >= ci
    lane = lax.broadcasted_iota(jnp.int32, (blk, V7X_LANES), 1)
    first = lane < SSD_HEADDIM
    keep_first = jnp.where(first, 1.0, 0.0).astype(BF16)
    keep_second = jnp.where(first, 0.0, 1.0).astype(BF16)
    dexp = dexp_ref[...]
    nw = nw_ref[...]
    n_heads = acum_ref.shape[0]
    gw = x_ref.shape[1]

    def head_selector(terms, lanes_per_head):
        k = lax.broadcasted_iota(jnp.int32, (terms * n_heads, n_heads * lanes_per_head), 0)
        n = lax.broadcasted_iota(jnp.int32, (terms * n_heads, n_heads * lanes_per_head), 1)
        shift = int(math.log2(lanes_per_head))
        return jnp.where((k & (n_heads - 1)) == jnp.right_shift(n, shift), 1.0, 0.0).astype(BF16)

    sel_mask = head_selector(3, blk)
    sel_chan = head_selector(2, SSD_HEADDIM)
    transposed_lhs = (((0,), (0,)), ((), ()))

    for sub in range(step_rows // blk):
        off = sub * blk
        src_r = src_ref[:, pl.ds(off, blk)]
        acum_r = acum_ref[:, pl.ds(off, blk)]
        acum_bc = lax.dot_general(_split_bf16(acum_r, 3), sel_mask, transposed_lhs,
                                  preferred_element_type=F32)
        in_weight = lax.dot_general(_split_bf16(inw_ref[:, pl.ds(off, blk)], 2), sel_chan,
                                    transposed_lhs, preferred_element_type=F32)
        xs = conv_silu(xe_ref, wx_ref, bx_ref, off)
        bmb = conv_silu(be_ref, wb_ref, bb_ref, off).astype(BF16)
        cmb = conv_silu(ce_ref, wc_ref, bc_ref, off).astype(BF16)

        cb = lax.dot_general(cmb, bmb, (((1,), (1,)), ((), ())), preferred_element_type=F32)
        state = state_ref[...]
        y_off = jnp.dot(cmb, state.astype(BF16), preferred_element_type=F32)

        y_tiles, decay_tiles = [], []
        for pair in range(gw // V7X_LANES):
            r0, r1 = 2 * pair, 2 * pair + 1
            cols = slice(pair * V7X_LANES, (pair + 1) * V7X_LANES)
            xp = xs[:, cols]
            xpb = xp.astype(BF16)
            out_decay = jnp.exp2(jnp.where(first, acum_bc[:, r0 * blk:r0 * blk + V7X_LANES],
                                           acum_bc[:, r1 * blk:r1 * blk + V7X_LANES]))
            y = y_off[:, cols] * out_decay + dexp[:, cols] * xp
            for r, keep in ((r0, keep_first), (r1, keep_second)):
                seg = acum_bc[:, r * blk:(r + 1) * blk] - src_r[r:r + 1, :]
                lmat = jnp.exp2(jnp.where(causal, seg, -jnp.inf))
                y = y + jnp.dot((cb * lmat).astype(BF16), xpb * keep,
                                preferred_element_type=F32)
            y_tiles.append(y)
            decay_tiles.append(out_decay[blk - 1:blk, :])
        y = jnp.concatenate(y_tiles, axis=1)
        xw = (xs * in_weight).astype(BF16)
        upd = lax.dot_general(bmb, xw, transposed_lhs, preferred_element_type=F32)
        state_ref[...] = state * jnp.concatenate(decay_tiles, axis=1) + upd

        yg = y * _silu(z_ref[pl.ds(off, blk), :].astype(F32))
        o_ref[pl.ds(off, blk), :] = (yg * _rms_scale(yg) * nw).astype(o_ref.dtype)

    xe_ref[0:halo, :] = xe_ref[step_rows:step_rows + halo, :]
    be_ref[0:halo, :] = be_ref[step_rows:step_rows + halo, :]
    ce_ref[0:halo, :] = ce_ref[step_rows:step_rows + halo, :]


def ssd_core(zxbc, dt_raw_t, conv_w, conv_b, dt_bias, a_log, d_skip, norm_w, *, batch, seq, name):
    t = zxbc.shape[0]
    g, r, n, gw = SSD_GROUPS, SSD_HEADS_PER_GROUP, SSD_STATE, SSD_GROUP_WIDTH
    d_inner = g * gw
    step, blk = SSD_STEP, SSD_BLK
    steps = seq // step
    x_off = d_inner // gw
    b_off = 2 * d_inner // n
    c_off = b_off + g
    wb_off = d_inner // n
    wc_off = wb_off + g

    src_r, acum_r, inw_r = (a.reshape(g, r, t) for a in
                           ssd_decays(dt_raw_t, dt_bias, a_log, blk=blk, name=name + "_decays"))
    dexp = jnp.repeat(d_skip, SSD_HEADDIM).reshape(1, d_inner)
    conv_b = conv_b.reshape(1, -1)

    row = lambda b, gi, c: b * steps + c
    return pl.pallas_call(
        functools.partial(_ssd_kernel, blk=blk),
        grid=(batch, g, steps),
        in_specs=[pl.BlockSpec((step, gw), lambda b, gi, c: (row(b, gi, c), gi)),
                  pl.BlockSpec((step, gw), lambda b, gi, c: (row(b, gi, c), x_off + gi)),
                  pl.BlockSpec((step, n), lambda b, gi, c: (row(b, gi, c), b_off + gi)),
                  pl.BlockSpec((step, n), lambda b, gi, c: (row(b, gi, c), c_off + gi)),
                  pl.BlockSpec((SSD_CONV_W, gw), lambda b, gi, c: (0, gi)),
                  pl.BlockSpec((SSD_CONV_W, n), lambda b, gi, c: (0, wb_off + gi)),
                  pl.BlockSpec((SSD_CONV_W, n), lambda b, gi, c: (0, wc_off + gi)),
                  pl.BlockSpec((1, gw), lambda b, gi, c: (0, gi)),
                  pl.BlockSpec((1, n), lambda b, gi, c: (0, wb_off + gi)),
                  pl.BlockSpec((1, n), lambda b, gi, c: (0, wc_off + gi)),
                  pl.BlockSpec((None, r, step), lambda b, gi, c: (gi, 0, row(b, gi, c))),
                  pl.BlockSpec((None, r, step), lambda b, gi, c: (gi, 0, row(b, gi, c))),
                  pl.BlockSpec((None, r, step), lambda b, gi, c: (gi, 0, row(b, gi, c))),
                  pl.BlockSpec((1, gw), lambda b, gi, c: (0, gi)),
                  pl.BlockSpec((1, gw), lambda b, gi, c: (0, gi))],
        out_specs=pl.BlockSpec((step, gw), lambda b, gi, c: (row(b, gi, c), gi)),
        out_shape=jax.ShapeDtypeStruct((t, d_inner), BF16),
        scratch_shapes=[pltpu.VMEM((n, gw), F32),
                        pltpu.VMEM((step + V7X_SUBLANES, gw), F32),
                        pltpu.VMEM((step + V7X_SUBLANES, n), F32),
                        pltpu.VMEM((step + V7X_SUBLANES, n), F32)],
        compiler_params=_params(("parallel", "parallel", "arbitrary")),
        name=name,
    )(zxbc, zxbc, zxbc, zxbc, conv_w, conv_w, conv_w, conv_b, conv_b, conv_b,
      src_r, acum_r, inw_r, dexp, norm_w.reshape(1, d_inner))


def kernel(x, norm_mix_pre, norm_mix_post, norm_ffn_pre, norm_ffn_post, ret_w_in, ret_gn_w, ret_w_out, ssd_w_in, ssd_conv_w, ssd_conv_b, ssd_dt_bias, ssd_a_log, ssd_d, ssd_norm_w, ssd_w_out, mlp_w_up, mlp_w_down):
    batch, seq, d = x.shape
    h = x.reshape(batch * seq, d)

    n_zxbc = ssd_w_in.shape[2] - SSD_GROUPS * SSD_HEADS_PER_GROUP
    ssd_w_in_t = jnp.swapaxes(ssd_w_in, 1, 2)
    ret_w_in, ret_w_out, ssd_w_out = (w.astype(BF16) for w in (ret_w_in, ret_w_out, ssd_w_out))

    proj = norm_matmul(h, norm_mix_pre, 0, ret_w_in, 0, out_dtype=BF16, name="ret_in_proj")
    y = retention_core(proj, ret_gn_w[0], batch=batch, seq=seq, name="retention_core")
    h, u = out_proj_residual(y, ret_w_out, 0, h, norm_mix_post, 0, norm_ffn_pre, 0,
                             name="ret_out_proj")
    h, u = mlp_residual(u, h, mlp_w_up, mlp_w_down, norm_ffn_post, 0, norm_mix_pre, 1, name="mlp0")

    zxbc, dt_raw_t = matmul_dt(u, ssd_w_in_t, 0, n_out=n_zxbc, out_dtype=F32,
                               name="ssd_in_proj")
    y = ssd_core(zxbc, dt_raw_t, ssd_conv_w[0], ssd_conv_b[0], ssd_dt_bias[0], ssd_a_log[0],
                 ssd_d[0], ssd_norm_w[0], batch=batch, seq=seq, name="ssd_core")
    h, u = out_proj_residual(y, ssd_w_out, 0, h, norm_mix_post, 1, norm_ffn_pre, 1,
                             name="ssd_out_proj")
    (h,) = mlp_residual(u, h, mlp_w_up, mlp_w_down, norm_ffn_post, 1, name="mlp1")
    return h.reshape(batch, seq, d)
```

```python
import functools
import math

import jax
import jax.numpy as jnp
from jax import lax
from jax.experimental import pallas as pl
from jax.experimental.pallas import tpu as pltpu

F32 = jnp.float32
BF16 = jnp.bfloat16

RMS_EPS = 1e-6
GN_EPS = 1e-5
ROPE_BASE = 10000.0
CHUNK = 64
LOG2_E = math.log2(math.e)

RET_HEAD_DK = 256
RET_HEAD_DV = 512
SSD_HEADDIM = 64
SSD_GROUPS = 8
SSD_HEADS_PER_GROUP = 8
SSD_STATE = 128
SSD_CONV_W = 4
SSD_GROUP_WIDTH = SSD_HEADS_PER_GROUP * SSD_HEADDIM

V7X_LANES = 128
V7X_SUBLANES = 8
V7X_VMEM_LIMIT_BYTES = 56 * 1024 * 1024

PROJ_TM, PROJ_TN = 1024, 2048
PROJ_WCAST_TN = 1024
OUT_TM, OUT_TK = 512, 2048
MLP_TM, MLP_TF = 1024, 1024
MLP_FIRST_TF = 512
RET_STEP, RET_BLK = 2048, 256
SSD_DECAY_STEP = 2048
SSD_STEP, SSD_BLK = 2048, 128


def _params(semantics):
    return pltpu.CompilerParams(dimension_semantics=semantics,
                                vmem_limit_bytes=V7X_VMEM_LIMIT_BYTES)


def _rms_scale(x):
    return lax.rsqrt(jnp.mean(x * x, axis=-1, keepdims=True) + RMS_EPS)


def _silu(x):
    half = 0.5 * x
    return half + half * jnp.tanh(half)


def _norm_matmul_kernel(x_ref, g_ref, w_ref, o_ref, u_ref):
    @pl.when(pl.program_id(1) == 0)
    def _():
        x = x_ref[...]
        u_ref[...] = (x * _rms_scale(x) * g_ref[...]).astype(BF16)

    o_ref[...] = jnp.dot(u_ref[...], w_ref[...],
                         preferred_element_type=F32).astype(o_ref.dtype)


def _matmul_dt_kernel(u_ref, wt_ref, wdt_ref, o_ref, dt_ref, wb_ref):
    contract_last = (((1,), (1,)), ((), ()))

    @pl.when(pl.program_id(1) == 0)
    def _():
        wb_ref[...] = wt_ref[...].astype(BF16)

    @pl.when(pl.program_id(0) == 0)
    def _():
        dt_ref[...] = lax.dot_general(wdt_ref[...].astype(BF16), u_ref[...], contract_last,
                                      preferred_element_type=F32)

    o_ref[...] = lax.dot_general(u_ref[...], wb_ref[...], contract_last,
                                 preferred_element_type=F32).astype(o_ref.dtype)


def _gain_spec(d, layer):
    return pl.BlockSpec((None, 1, d), lambda *_: (layer, 0, 0))


def _gains(g):
    return g.reshape(g.shape[0], 1, g.shape[1])


def norm_matmul(h, g, layer, w, w_layer, *, out_dtype, name):
    t, d = h.shape
    n = w.shape[2]
    tm, tn = PROJ_TM, PROJ_TN
    return pl.pallas_call(
        _norm_matmul_kernel,
        grid=(t // tm, n // tn),
        in_specs=[pl.BlockSpec((tm, d), lambda i, j: (i, 0)),
                  _gain_spec(d, layer),
                  pl.BlockSpec((None, d, tn), lambda i, j: (w_layer, 0, j))],
        out_specs=pl.BlockSpec((tm, tn), lambda i, j: (i, j)),
        out_shape=jax.ShapeDtypeStruct((t, n), out_dtype),
        scratch_shapes=[pltpu.VMEM((tm, d), BF16)],
        compiler_params=_params(("parallel", "arbitrary")),
        name=name,
    )(h, _gains(g), w)


def matmul_dt(u, w_t, w_layer, *, n_out, out_dtype, name):
    t, d = u.shape
    n_dt = w_t.shape[1] - n_out
    tm, tn = PROJ_TM, PROJ_WCAST_TN
    n_rows = t // tm
    dt_rows = n_out // n_dt
    dt_block = lambda j, i: (0, jnp.where(j == 0, i, n_rows - 1))
    return pl.pallas_call(
        _matmul_dt_kernel,
        grid=(n_out // tn, n_rows),
        in_specs=[pl.BlockSpec((tm, d), lambda j, i: (i, 0)),
                  pl.BlockSpec((None, tn, d), lambda j, i: (w_layer, j, 0)),
                  pl.BlockSpec((None, n_dt, d), lambda j, i: (w_layer, dt_rows, 0))],
        out_specs=[pl.BlockSpec((tm, tn), lambda j, i: (i, j)),
                   pl.BlockSpec((n_dt, tm), dt_block)],
        out_shape=[jax.ShapeDtypeStruct((t, n_out), out_dtype),
                   jax.ShapeDtypeStruct((n_dt, t), F32)],
        scratch_shapes=[pltpu.VMEM((tn, d), BF16)],
        compiler_params=_params(("arbitrary", "arbitrary")),
        name=name,
    )(u, w_t, w_t)


def _residual_norm(h, m, g_post, g_next=None):
    hn = h + m * _rms_scale(m) * g_post
    if g_next is None:
        return hn, None
    return hn, (hn * _rms_scale(hn) * g_next).astype(BF16)


def _lagged_residual_steps(accumulate, acc_ref, seed_ref, h_ref, gpost_ref, gnext_ref, o_ref,
                           un_ref, n_slices):
    i = pl.program_id(0)
    j = pl.program_id(1)
    last = pl.num_programs(0) - 1
    first = 0 if seed_ref is None else 1
    slice_rows = acc_ref.shape[1] // n_slices

    def add(part):
        slot = (i + first) % 2
        acc_ref[slot] = jnp.where(j == 0, part, acc_ref[slot] + part)

    def finish_slice(m):
        hn, un = _residual_norm(h_ref[...], m, gpost_ref[...],
                                None if gnext_ref is None else gnext_ref[...])
        o_ref[...] = hn
        if un_ref is not None:
            un_ref[...] = un

    def finish_from_acc():
        rows = pl.ds(pl.multiple_of(j * slice_rows, slice_rows), slice_rows)
        finish_slice(acc_ref[(i + first + 1) % 2, rows, :])

    @pl.when((i == 0) & (j == 0))
    def _():
        acc_ref[...] = jnp.zeros_like(acc_ref)

    @pl.when(i == 0)
    def _():
        if seed_ref is not None:
            finish_slice(seed_ref[...])
        accumulate(add)

    @pl.when((i > 0) & (i < last))
    def _():
        finish_from_acc()
        accumulate(add)

    @pl.when(i == last)
    def _():
        finish_from_acc()


def _lagged_tile_maps(n_tiles, n_steps, seeded):
    first = 1 if seeded else 0
    rows = n_tiles - first + 1
    tile_in = lambda i, j: jnp.minimum(i + first, n_tiles - 1)
    slice_out = lambda i, j: jnp.where(i + first == 0, 0, (i + first - 1) * n_steps + j)
    step = lambda i, j: jnp.where(i < rows - 1, j, n_steps - 1)
    return rows, tile_in, slice_out, step


def _out_proj_kernel(y_ref, w_ref, h_ref, gpost_ref, gnext_ref, o_ref, un_ref, acc_ref, *, n_k):
    tk = y_ref.shape[1]

    def accumulate(add):
        w_rows = pl.ds(pl.multiple_of(pl.program_id(1) * tk, tk), tk)
        add(jnp.dot(y_ref[...], w_ref[w_rows, :], preferred_element_type=F32))

    _lagged_residual_steps(accumulate, acc_ref, None, h_ref, gpost_ref, gnext_ref, o_ref, un_ref,
                           n_k)


def out_proj_residual(y, w, w_layer, h, g_post, layer, g_next, next_layer, *, name):
    t, k = y.shape
    d = w.shape[2]
    tm, tk = OUT_TM, OUT_TK
    n, n_k = t // tm, k // tk
    rows, tile_in, slice_out, step = _lagged_tile_maps(n, n_k, seeded=False)
    finished = pl.BlockSpec((tm // n_k, d), lambda i, j: (slice_out(i, j), 0))
    return pl.pallas_call(
        functools.partial(_out_proj_kernel, n_k=n_k),
        grid=(rows, n_k),
        in_specs=[pl.BlockSpec((tm, tk), lambda i, j: (tile_in(i, j), step(i, j))),
                  pl.BlockSpec((None, k, d), lambda i, j: (w_layer, 0, 0),
                               pipeline_mode=pl.Buffered(1)),
                  finished,
                  _gain_spec(d, layer),
                  _gain_spec(d, next_layer)],
        out_specs=[finished, finished],
        out_shape=[jax.ShapeDtypeStruct((t, d), F32), jax.ShapeDtypeStruct((t, d), BF16)],
        scratch_shapes=[pltpu.VMEM((2, tm, d), F32)],
        compiler_params=_params(("arbitrary", "arbitrary")),
        name=name,
    )(y, w, h, _gains(g_post), _gains(g_next))


def _mlp_product(u, wup, wdn):
    hid = jnp.maximum(jnp.dot(u, wup, preferred_element_type=F32), 0.0)
    return jnp.dot((hid * hid).astype(BF16), wdn, preferred_element_type=F32)


def _mlp_first_tile_kernel(u_ref, wup_ref, wdn_ref, acc_ref, wupb_ref, wdnb_ref):
    wup = wup_ref[...].astype(BF16)
    wdn = wdn_ref[...].astype(BF16)
    wupb_ref[...] = wup
    wdnb_ref[...] = wdn

    @pl.when(pl.program_id(0) == 0)
    def _():
        acc_ref[...] = jnp.zeros_like(acc_ref)

    acc_ref[...] += _mlp_product(u_ref[...], wup, wdn)


def mlp_first_tile(u, wup, wdn, layer, *, name):
    t, d = u.shape
    f = wup.shape[2]
    tm, tf = MLP_TM, MLP_FIRST_TF
    return pl.pallas_call(
        _mlp_first_tile_kernel,
        grid=(f // tf,),
        in_specs=[pl.BlockSpec((tm, d), lambda j: (0, 0)),
                  pl.BlockSpec((None, d, tf), lambda j: (layer, 0, j)),
                  pl.BlockSpec((None, tf, d), lambda j: (layer, j, 0))],
        out_specs=[pl.BlockSpec((tm, d), lambda j: (0, 0)),
                   pl.BlockSpec((d, tf), lambda j: (0, j)),
                   pl.BlockSpec((tf, d), lambda j: (j, 0))],
        out_shape=[jax.ShapeDtypeStruct((tm, d), F32),
                   jax.ShapeDtypeStruct((d, f), BF16),
                   jax.ShapeDtypeStruct((f, d), BF16)],
        compiler_params=_params(("arbitrary",)),
        name=name,
    )(u, wup, wdn)


def _mlp_kernel(u_ref, seed_ref, h_ref, wup_ref, wdn_ref, gpost_ref, *rest, n_f, emit_u):
    if emit_u:
        gnext_ref, o_ref, un_ref, acc_ref = rest
    else:
        (o_ref, acc_ref), gnext_ref, un_ref = rest, None, None

    def accumulate(add):
        add(_mlp_product(u_ref[...], wup_ref[...], wdn_ref[...]))

    _lagged_residual_steps(accumulate, acc_ref, seed_ref, h_ref, gpost_ref, gnext_ref, o_ref,
                           un_ref, n_f)


def mlp_residual(u, h, wup, wdn, gpost, layer, g_next=None, next_layer=None, *, name):
    t, d = h.shape
    f = wup.shape[2]
    tm, tf = MLP_TM, MLP_TF
    n, n_f = t // tm, f // tf
    emit_u = g_next is not None
    seed, wup, wdn = mlp_first_tile(u, wup, wdn, layer, name=name + "_first")
    rows, tile_in, slice_out, step = _lagged_tile_maps(n, n_f, seeded=True)
    finished = pl.BlockSpec((tm // n_f, d), lambda i, j: (slice_out(i, j), 0))
    in_specs = [pl.BlockSpec((tm, d), lambda i, j: (tile_in(i, j), 0)),
                pl.BlockSpec((tm // n_f, d), lambda i, j: (jnp.where(i == 0, j, n_f - 1), 0)),
                finished,
                pl.BlockSpec((d, tf), lambda i, j: (0, step(i, j))),
                pl.BlockSpec((tf, d), lambda i, j: (step(i, j), 0)),
                _gain_spec(d, layer)]
    args = [u, seed, h, wup, wdn, _gains(gpost)]
    out_specs = [finished]
    out_shape = [jax.ShapeDtypeStruct((t, d), F32)]
    if emit_u:
        in_specs.append(_gain_spec(d, next_layer))
        args.append(_gains(g_next))
        out_specs.append(finished)
        out_shape.append(jax.ShapeDtypeStruct((t, d), BF16))
    return pl.pallas_call(
        functools.partial(_mlp_kernel, n_f=n_f, emit_u=emit_u),
        grid=(rows, n_f),
        in_specs=in_specs,
        out_specs=out_specs,
        out_shape=out_shape,
        scratch_shapes=[pltpu.VMEM((2, tm, d), F32)],
        compiler_params=_params(("arbitrary", "arbitrary")),
        name=name,
    )(*args)


def _retention_kernel(tab_ref, q_ref, k_ref, v_ref, g_ref, cos_ref, sin_ref, gnw_ref,
                      o_ref, state_ref, dmask_ref, xi_ref, zeta_ref, *, n_heads, blk):
    head = pl.program_id(2)
    half = RET_HEAD_DK // 2
    log_gamma = tab_ref[head]
    gamma_blk = tab_ref[n_heads + head]

    @pl.when(pl.program_id(1) == 0)
    def _():
        state_ref[head] = jnp.zeros(state_ref.shape[1:], F32)
        ri = lax.broadcasted_iota(jnp.int32, (blk, blk), 0)
        ci = lax.broadcasted_iota(jnp.int32, (blk, blk), 1)
        dist = jnp.abs(ri - ci).astype(F32)
        shift = int(math.log2(CHUNK))
        visible = jnp.right_shift(ci, shift) <= jnp.right_shift(ri, shift)
        dmask_ref[head] = jnp.where(visible, jnp.exp(dist * log_gamma), 0.0)
        pos = lax.broadcasted_iota(jnp.int32, (blk, V7X_LANES), 0).astype(F32)
        xi_ref[head] = jnp.exp((pos + 1.0) * log_gamma)
        zeta_ref[head] = jnp.exp((blk - 1.0 - pos) * log_gamma)

    xi = jnp.concatenate([xi_ref[head]] * (RET_HEAD_DV // V7X_LANES), axis=1)
    zeta = jnp.concatenate([zeta_ref[head]] * (RET_HEAD_DK // V7X_LANES), axis=1)
    dmask = dmask_ref[head]
    gnw = gnw_ref[...]

    for sub in range(q_ref.shape[0] // blk):
        rows = pl.ds(sub * blk, blk)
        cos = cos_ref[rows, :]
        sin = sin_ref[rows, :]

        def rotary(t):
            t1, t2 = t[:, :half], t[:, half:]
            return jnp.concatenate([t1 * cos - t2 * sin, t1 * sin + t2 * cos], axis=1)

        qr = rotary(q_ref[rows, :].astype(F32))
        kr = rotary(k_ref[rows, :].astype(F32)) * (RET_HEAD_DK ** -0.5)
        qb = qr.astype(BF16)
        kb = kr.astype(BF16)
        vb = v_ref[rows, :].astype(BF16)

        scores = lax.dot_general(qb, kb, (((1,), (1,)), ((), ())), preferred_element_type=F32)
        inner = jnp.dot((scores * dmask).astype(BF16), vb, preferred_element_type=F32)
        state = state_ref[head]
        cross = jnp.dot(qb, state.astype(BF16), preferred_element_type=F32)
        o = inner + cross * xi

        kz = (kr * zeta).astype(BF16)
        upd = lax.dot_general(kz, vb, (((0,), (0,)), ((), ())), preferred_element_type=F32)
        state_ref[head] = state * gamma_blk + upd

        mu = jnp.mean(o, axis=-1, keepdims=True)
        dev = o - mu
        var = jnp.mean(dev * dev, axis=-1, keepdims=True)
        normed = dev * lax.rsqrt(var + GN_EPS) * gnw
        o_ref[rows, :] = (_silu(g_ref[rows, :].astype(F32)) * normed).astype(o_ref.dtype)


def _rotary_tables(seq, half, blk):
    inv_freq = ROPE_BASE ** (-jnp.arange(half, dtype=F32) / half)
    ang_blk = (jnp.arange(seq // blk) * blk).astype(F32)[:, None] * inv_freq[None, :]
    ang_in = jnp.arange(blk).astype(F32)[:, None] * inv_freq[None, :]
    cb, sb = jnp.cos(ang_blk)[:, None, :], jnp.sin(ang_blk)[:, None, :]
    ci, si = jnp.cos(ang_in)[None], jnp.sin(ang_in)[None]
    return (cb * ci - sb * si).reshape(seq, half), (sb * ci + cb * si).reshape(seq, half)


def retention_core(proj, gn_w, *, batch, seq, name):
    t = proj.shape[0]
    n_heads = proj.shape[1] // (2 * RET_HEAD_DK + 2 * RET_HEAD_DV)
    step, blk = RET_STEP, RET_BLK
    steps = seq // step
    dv_per_dk = RET_HEAD_DV // RET_HEAD_DK
    k_off = n_heads
    v_off = 2 * n_heads // dv_per_dk
    g_off = v_off + n_heads

    half = RET_HEAD_DK // 2
    cos, sin = _rotary_tables(seq, half, blk)
    log_gamma = jnp.log1p(-jnp.exp2(-5.0 - jnp.arange(n_heads, dtype=F32)))
    tab = jnp.concatenate([log_gamma, jnp.exp(blk * log_gamma)])

    row = lambda b, c, h: b * steps + c
    return pl.pallas_call(
        functools.partial(_retention_kernel, n_heads=n_heads, blk=blk),
        grid=(batch, steps, n_heads),
        in_specs=[pl.BlockSpec(memory_space=pltpu.SMEM),
                  pl.BlockSpec((step, RET_HEAD_DK), lambda b, c, h: (row(b, c, h), h)),
                  pl.BlockSpec((step, RET_HEAD_DK), lambda b, c, h: (row(b, c, h), k_off + h)),
                  pl.BlockSpec((step, RET_HEAD_DV), lambda b, c, h: (row(b, c, h), v_off + h)),
                  pl.BlockSpec((step, RET_HEAD_DV), lambda b, c, h: (row(b, c, h), g_off + h)),
                  pl.BlockSpec((step, half), lambda b, c, h: (c, 0)),
                  pl.BlockSpec((step, half), lambda b, c, h: (c, 0)),
                  pl.BlockSpec((1, RET_HEAD_DV), lambda b, c, h: (0, h))],
        out_specs=pl.BlockSpec((step, RET_HEAD_DV), lambda b, c, h: (row(b, c, h), h)),
        out_shape=jax.ShapeDtypeStruct((t, n_heads * RET_HEAD_DV), BF16),
        scratch_shapes=[pltpu.VMEM((n_heads, RET_HEAD_DK, RET_HEAD_DV), F32),
                        pltpu.VMEM((n_heads, blk, blk), F32),
                        pltpu.VMEM((n_heads, blk, V7X_LANES), F32),
                        pltpu.VMEM((n_heads, blk, V7X_LANES), F32)],
        compiler_params=_params(("parallel", "arbitrary", "arbitrary")),
        name=name,
    )(tab, proj, proj, proj, proj, cos, sin, gn_w.reshape(1, -1))


def _softplus(x):
    return jnp.maximum(x, 0.0) + jnp.log1p(jnp.exp(-jnp.abs(x)))


def _split_bf16(v, terms):
    parts = []
    for _ in range(terms - 1):
        part = v.astype(BF16)
        parts.append(part)
        v = v - part.astype(F32)
    parts.append(v.astype(BF16))
    return jnp.concatenate(parts, axis=0)


def _sum_terms(stacked, rows):
    out = stacked[0:rows]
    for i in range(1, stacked.shape[0] // rows):
        out = out + stacked[i * rows:(i + 1) * rows]
    return out


def _ssd_decay_kernel(dt_ref, prm_ref, src_ref, acum_ref, inw_ref, *, blk):
    prm = prm_ref[...]
    a2 = -jnp.exp(prm[:, 1:2]) * LOG2_E
    ri = lax.broadcasted_iota(jnp.int32, (blk, blk), 0)
    ci = lax.broadcasted_iota(jnp.int32, (blk, blk), 1)
    upper_ones = jnp.where(ri <= ci, 1.0, 0.0).astype(BF16)
    n_heads = dt_ref.shape[0]
    for sub in range(dt_ref.shape[1] // blk):
        cols = pl.ds(sub * blk, blk)
        dt = _softplus(dt_ref[:, cols] + prm[:, 0:1])
        acum = _sum_terms(jnp.dot(_split_bf16(dt * a2, 3), upper_ones,
                                  preferred_element_type=F32), n_heads)
        src_ref[:, cols] = acum - jnp.log2(dt)
        acum_ref[:, cols] = acum
        inw_ref[:, cols] = dt * jnp.exp2(acum[:, blk - 1:blk] - acum)


def ssd_decays(dt_raw_t, dt_bias, a_log, *, blk, name):
    n_heads, t = dt_raw_t.shape
    step = SSD_DECAY_STEP
    spec = pl.BlockSpec((n_heads, step), lambda i: (0, i))
    shape = jax.ShapeDtypeStruct((n_heads, t), F32)
    return pl.pallas_call(
        functools.partial(_ssd_decay_kernel, blk=blk),
        grid=(t // step,),
        in_specs=[spec, pl.BlockSpec((n_heads, 2), lambda i: (0, 0))],
        out_specs=[spec, spec, spec],
        out_shape=[shape, shape, shape],
        compiler_params=_params(("parallel",)),
        name=name,
    )(dt_raw_t, jnp.stack([dt_bias, a_log], axis=-1))


def _ssd_kernel(z_ref, x_ref, b_ref, c_ref, wx_ref, wb_ref, wc_ref, bx_ref, bb_ref, bc_ref,
                src_ref, acum_ref, inw_ref, dexp_ref, nw_ref, o_ref,
                state_ref, xe_ref, be_ref, ce_ref, *, blk):
    step_rows = x_ref.shape[0]
    halo = V7X_SUBLANES

    @pl.when(pl.program_id(2) == 0)
    def _():
        state_ref[...] = jnp.zeros_like(state_ref)
        xe_ref[0:halo, :] = jnp.zeros((halo, xe_ref.shape[1]), F32)
        be_ref[0:halo, :] = jnp.zeros((halo, be_ref.shape[1]), F32)
        ce_ref[0:halo, :] = jnp.zeros((halo, ce_ref.shape[1]), F32)

    xe_ref[halo:, :] = x_ref[...].astype(F32)
    be_ref[halo:, :] = b_ref[...].astype(F32)
    ce_ref[halo:, :] = c_ref[...].astype(F32)

    def conv_silu(ext_ref, w_ref, bias_ref, off):
        out = bias_ref[...] + ext_ref[pl.ds(halo + off, blk), :] * w_ref[SSD_CONV_W - 1:SSD_CONV_W, :]
        for s in range(1, SSD_CONV_W):
            out = out + (ext_ref[pl.ds(halo + off - s, blk), :]
                         * w_ref[SSD_CONV_W - 1 - s:SSD_CONV_W - s, :])
        return _silu(out)

    ri = lax.broadcasted_iota(jnp.int32, (blk, blk), 0)
    ci = lax.broadcasted_iota(jnp.int32, (blk, blk), 1)
    causal = ri >= ci
    lane = lax.broadcasted_iota(jnp.int32, (blk, V7X_LANES), 1)
    first = lane < SSD_HEADDIM
    keep_first = jnp.where(first, 1.0, 0.0).astype(BF16)
    keep_second = jnp.where(first, 0.0, 1.0).astype(BF16)
    dexp = dexp_ref[...]
    nw = nw_ref[...]
    n_heads = acum_ref.shape[0]
    gw = x_ref.shape[1]

    def head_selector(terms, lanes_per_head):
        k = lax.broadcasted_iota(jnp.int32, (terms * n_heads, n_heads * lanes_per_head), 0)
        n = lax.broadcasted_iota(jnp.int32, (terms * n_heads, n_heads * lanes_per_head), 1)
        shift = int(math.log2(lanes_per_head))
        return jnp.where((k & (n_heads - 1)) == jnp.right_shift(n, shift), 1.0, 0.0).astype(BF16)

    sel_mask = head_selector(3, blk)
    sel_chan = head_selector(2, SSD_HEADDIM)
    transposed_lhs = (((0,), (0,)), ((), ()))

    for sub in range(step_rows // blk):
        off = sub * blk
        src_r = src_ref[:, pl.ds(off, blk)]
        acum_r = acum_ref[:, pl.ds(off, blk)]
        acum_bc = lax.dot_general(_split_bf16(acum_r, 3), sel_mask, transposed_lhs,
                                  preferred_element_type=F32)
        in_weight = lax.dot_general(_split_bf16(inw_ref[:, pl.ds(off, blk)], 2), sel_chan,
                                    transposed_lhs, preferred_element_type=F32)
        xs = conv_silu(xe_ref, wx_ref, bx_ref, off)
        bmb = conv_silu(be_ref, wb_ref, bb_ref, off).astype(BF16)
        cmb = conv_silu(ce_ref, wc_ref, bc_ref, off).astype(BF16)

        cb = lax.dot_general(cmb, bmb, (((1,), (1,)), ((), ())), preferred_element_type=F32)
        state = state_ref[...]
        y_off = jnp.dot(cmb, state.astype(BF16), preferred_element_type=F32)

        y_tiles, decay_tiles = [], []
        for pair in range(gw // V7X_LANES):
            r0, r1 = 2 * pair, 2 * pair + 1
            cols = slice(pair * V7X_LANES, (pair + 1) * V7X_LANES)
            xp = xs[:, cols]
            xpb = xp.astype(BF16)
            out_decay = jnp.exp2(jnp.where(first, acum_bc[:, r0 * blk:r0 * blk + V7X_LANES],
                                           acum_bc[:, r1 * blk:r1 * blk + V7X_LANES]))
            y = y_off[:, cols] * out_decay + dexp[:, cols] * xp
            for r, keep in ((r0, keep_first), (r1, keep_second)):
                seg = acum_bc[:, r * blk:(r + 1) * blk] - src_r[r:r + 1, :]
                lmat = jnp.exp2(jnp.where(causal, seg, -jnp.inf))
                y = y + jnp.dot((cb * lmat).astype(BF16), xpb * keep,
                                preferred_element_type=F32)
            y_tiles.append(y)
            decay_tiles.append(out_decay[blk - 1:blk, :])
        y = jnp.concatenate(y_tiles, axis=1)
        xw = (xs * in_weight).astype(BF16)
        upd = lax.dot_general(bmb, xw, transposed_lhs, preferred_element_type=F32)
        state_ref[...] = state * jnp.concatenate(decay_tiles, axis=1) + upd

        yg = y * _silu(z_ref[pl.ds(off, blk), :].astype(F32))
        o_ref[pl.ds(off, blk), :] = (yg * _rms_scale(yg) * nw).astype(o_ref.dtype)

    xe_ref[0:halo, :] = xe_ref[step_rows:step_rows + halo, :]
    be_ref[0:halo, :] = be_ref[step_rows:step_rows + halo, :]
    ce_ref[0:halo, :] = ce_ref[step_rows:step_rows + halo, :]


def ssd_core(zxbc, dt_raw_t, conv_w, conv_b, dt_bias, a_log, d_skip, norm_w, *, batch, seq, name):
    t = zxbc.shape[0]
    g, r, n, gw = SSD_GROUPS, SSD_HEADS_PER_GROUP, SSD_STATE, SSD_GROUP_WIDTH
    d_inner = g * gw
    step, blk = SSD_STEP, SSD_BLK
    steps = seq // step
    x_off = d_inner // gw
    b_off = 2 * d_inner // n
    c_off = b_off + g
    wb_off = d_inner // n
    wc_off = wb_off + g

    src_r, acum_r, inw_r = (a.reshape(g, r, t) for a in
                           ssd_decays(dt_raw_t, dt_bias, a_log, blk=blk, name=name + "_decays"))
    dexp = jnp.repeat(d_skip, SSD_HEADDIM).reshape(1, d_inner)
    conv_b = conv_b.reshape(1, -1)

    row = lambda b, gi, c: b * steps + c
    return pl.pallas_call(
        functools.partial(_ssd_kernel, blk=blk),
        grid=(batch, g, steps),
        in_specs=[pl.BlockSpec((step, gw), lambda b, gi, c: (row(b, gi, c), gi)),
                  pl.BlockSpec((step, gw), lambda b, gi, c: (row(b, gi, c), x_off + gi)),
                  pl.BlockSpec((step, n), lambda b, gi, c: (row(b, gi, c), b_off + gi)),
                  pl.BlockSpec((step, n), lambda b, gi, c: (row(b, gi, c), c_off + gi)),
                  pl.BlockSpec((SSD_CONV_W, gw), lambda b, gi, c: (0, gi)),
                  pl.BlockSpec((SSD_CONV_W, n), lambda b, gi, c: (0, wb_off + gi)),
                  pl.BlockSpec((SSD_CONV_W, n), lambda b, gi, c: (0, wc_off + gi)),
                  pl.BlockSpec((1, gw), lambda b, gi, c: (0, gi)),
                  pl.BlockSpec((1, n), lambda b, gi, c: (0, wb_off + gi)),
                  pl.BlockSpec((1, n), lambda b, gi, c: (0, wc_off + gi)),
                  pl.BlockSpec((None, r, step), lambda b, gi, c: (gi, 0, row(b, gi, c))),
                  pl.BlockSpec((None, r, step), lambda b, gi, c: (gi, 0, row(b, gi, c))),
                  pl.BlockSpec((None, r, step), lambda b, gi, c: (gi, 0, row(b, gi, c))),
                  pl.BlockSpec((1, gw), lambda b, gi, c: (0, gi)),
                  pl.BlockSpec((1, gw), lambda b, gi, c: (0, gi))],
        out_specs=pl.BlockSpec((step, gw), lambda b, gi, c: (row(b, gi, c), gi)),
        out_shape=jax.ShapeDtypeStruct((t, d_inner), BF16),
        scratch_shapes=[pltpu.VMEM((n, gw), F32),
                        pltpu.VMEM((step + V7X_SUBLANES, gw), F32),
                        pltpu.VMEM((step + V7X_SUBLANES, n), F32),
                        pltpu.VMEM((step + V7X_SUBLANES, n), F32)],
        compiler_params=_params(("parallel", "parallel", "arbitrary")),
        name=name,
    )(zxbc, zxbc, zxbc, zxbc, conv_w, conv_w, conv_w, conv_b, conv_b, conv_b,
      src_r, acum_r, inw_r, dexp, norm_w.reshape(1, d_inner))


def kernel(x, norm_mix_pre, norm_mix_post, norm_ffn_pre, norm_ffn_post, ret_w_in, ret_gn_w, ret_w_out, ssd_w_in, ssd_conv_w, ssd_conv_b, ssd_dt_bias, ssd_a_log, ssd_d, ssd_norm_w, ssd_w_out, mlp_w_up, mlp_w_down):
    batch, seq, d = x.shape
    h = x.reshape(batch * seq, d)

    n_zxbc = ssd_w_in.shape[2] - SSD_GROUPS * SSD_HEADS_PER_GROUP
    ssd_w_in_t = jnp.swapaxes(ssd_w_in, 1, 2)
    ret_w_in, ret_w_out, ssd_w_out = (w.astype(BF16) for w in (ret_w_in, ret_w_out, ssd_w_out))

    proj = norm_matmul(h, norm_mix_pre, 0, ret_w_in, 0, out_dtype=BF16, name="ret_in_proj")
    y = retention_core(proj, ret_gn_w[0], batch=batch, seq=seq, name="retention_core")
    h, u = out_proj_residual(y, ret_w_out, 0, h, norm_mix_post, 0, norm_ffn_pre, 0,
                             name="ret_out_proj")
    h, u = mlp_residual(u, h, mlp_w_up, mlp_w_down, norm_ffn_post, 0, norm_mix_pre, 1, name="mlp0")

    zxbc, dt_raw_t = matmul_dt(u, ssd_w_in_t, 0, n_out=n_zxbc, out_dtype=BF16,
                               name="ssd_in_proj")
    y = ssd_core(zxbc, dt_raw_t, ssd_conv_w[0], ssd_conv_b[0], ssd_dt_bias[0], ssd_a_log[0],
                 ssd_d[0], ssd_norm_w[0], batch=batch, seq=seq, name="ssd_core")
    h, u = out_proj_residual(y, ssd_w_out, 0, h, norm_mix_post, 1, norm_ffn_pre, 1,
                             name="ssd_out_proj")
    (h,) = mlp_residual(u, h, mlp_w_up, mlp_w_down, norm_ffn_post, 1, name="mlp1")
    return h.reshape(batch, seq, d)
```

```python
import functools
import math

import jax
import jax.numpy as jnp
from jax import lax
from jax.experimental import pallas as pl
from jax.experimental.pallas import tpu as pltpu

F32 = jnp.float32
BF16 = jnp.bfloat16

RMS_EPS = 1e-6
GN_EPS = 1e-5
ROPE_BASE = 10000.0
CHUNK = 64
LOG2_E = math.log2(math.e)

RET_HEAD_DK = 256
RET_HEAD_DV = 512
SSD_HEADDIM = 64
SSD_GROUPS = 8
SSD_HEADS_PER_GROUP = 8
SSD_STATE = 128
SSD_CONV_W = 4
SSD_GROUP_WIDTH = SSD_HEADS_PER_GROUP * SSD_HEADDIM

V7X_LANES = 128
V7X_SUBLANES = 8
V7X_VMEM_LIMIT_BYTES = 56 * 1024 * 1024

PROJ_TM, PROJ_TN = 1024, 2048
PROJ_WCAST_TN = 1024
OUT_TM, OUT_TK = 512, 2048
MLP_TM, MLP_TF = 1024, 1024
MLP_FIRST_TF = 512
RET_STEP, RET_BLK = 2048, 256
SSD_DECAY_STEP = 2048
SSD_STEP, SSD_BLK = 2048, 128


def _params(semantics):
    return pltpu.CompilerParams(dimension_semantics=semantics,
                                vmem_limit_bytes=V7X_VMEM_LIMIT_BYTES)


def _rms_scale(x):
    return lax.rsqrt(jnp.mean(x * x, axis=-1, keepdims=True) + RMS_EPS)


def _silu(x):
    half = 0.5 * x
    return half + half * jnp.tanh(half)


def _norm_matmul_kernel(x_ref, g_ref, w_ref, o_ref, u_ref):
    @pl.when(pl.program_id(1) == 0)
    def _():
        x = x_ref[...]
        u_ref[...] = (x * _rms_scale(x) * g_ref[...]).astype(BF16)

    o_ref[...] = jnp.dot(u_ref[...], w_ref[...],
                         preferred_element_type=F32).astype(o_ref.dtype)


def _matmul_dt_kernel(u_ref, wt_ref, wdt_ref, o_ref, dt_ref, wb_ref):
    contract_last = (((1,), (1,)), ((), ()))

    @pl.when(pl.program_id(1) == 0)
    def _():
        wb_ref[...] = wt_ref[...].astype(BF16)

    @pl.when(pl.program_id(0) == 0)
    def _():
        dt_ref[...] = lax.dot_general(wdt_ref[...].astype(BF16), u_ref[...], contract_last,
                                      preferred_element_type=F32)

    o_ref[...] = lax.dot_general(u_ref[...], wb_ref[...], contract_last,
                                 preferred_element_type=F32).astype(o_ref.dtype)


def _gain_spec(d, layer):
    return pl.BlockSpec((None, 1, d), lambda *_: (layer, 0, 0))


def _gains(g):
    return g.reshape(g.shape[0], 1, g.shape[1])


def norm_matmul(h, g, layer, w, w_layer, *, out_dtype, name):
    t, d = h.shape
    n = w.shape[2]
    tm, tn = PROJ_TM, PROJ_TN
    return pl.pallas_call(
        _norm_matmul_kernel,
        grid=(t // tm, n // tn),
        in_specs=[pl.BlockSpec((tm, d), lambda i, j: (i, 0)),
                  _gain_spec(d, layer),
                  pl.BlockSpec((None, d, tn), lambda i, j: (w_layer, 0, j))],
        out_specs=pl.BlockSpec((tm, tn), lambda i, j: (i, j)),
        out_shape=jax.ShapeDtypeStruct((t, n), out_dtype),
        scratch_shapes=[pltpu.VMEM((tm, d), BF16)],
        compiler_params=_params(("parallel", "arbitrary")),
        name=name,
    )(h, _gains(g), w)


def matmul_dt(u, w_t, w_layer, *, n_out, out_dtype, name):
    t, d = u.shape
    n_dt = w_t.shape[1] - n_out
    tm, tn = PROJ_TM, PROJ_WCAST_TN
    n_rows = t // tm
    dt_rows = n_out // n_dt
    dt_block = lambda j, i: (0, jnp.where(j == 0, i, n_rows - 1))
    return pl.pallas_call(
        _matmul_dt_kernel,
        grid=(n_out // tn, n_rows),
        in_specs=[pl.BlockSpec((tm, d), lambda j, i: (i, 0)),
                  pl.BlockSpec((None, tn, d), lambda j, i: (w_layer, j, 0)),
                  pl.BlockSpec((None, n_dt, d), lambda j, i: (w_layer, dt_rows, 0))],
        out_specs=[pl.BlockSpec((tm, tn), lambda j, i: (i, j)),
                   pl.BlockSpec((n_dt, tm), dt_block)],
        out_shape=[jax.ShapeDtypeStruct((t, n_out), out_dtype),
                   jax.ShapeDtypeStruct((n_dt, t), F32)],
        scratch_shapes=[pltpu.VMEM((tn, d), BF16)],
        compiler_params=_params(("arbitrary", "arbitrary")),
        name=name,
    )(u, w_t, w_t)


def _residual_norm(h, m, g_post, g_next=None):
    hn = h + m * _rms_scale(m) * g_post
    if g_next is None:
        return hn, None
    return hn, (hn * _rms_scale(hn) * g_next).astype(BF16)


def _lagged_residual_steps(accumulate, acc_ref, seed_ref, h_ref, gpost_ref, gnext_ref, o_ref,
                           un_ref, n_slices):
    i = pl.program_id(0)
    j = pl.program_id(1)
    last = pl.num_programs(0) - 1
    first = 0 if seed_ref is None else 1
    slice_rows = acc_ref.shape[1] // n_slices

    def add(part):
        slot = (i + first) % 2
        acc_ref[slot] = jnp.where(j == 0, part, acc_ref[slot] + part)

    def finish_slice(m):
        hn, un = _residual_norm(h_ref[...], m, gpost_ref[...],
                                None if gnext_ref is None else gnext_ref[...])
        o_ref[...] = hn
        if un_ref is not None:
            un_ref[...] = un

    def finish_from_acc():
        rows = pl.ds(pl.multiple_of(j * slice_rows, slice_rows), slice_rows)
        finish_slice(acc_ref[(i + first + 1) % 2, rows, :])

    @pl.when((i == 0) & (j == 0))
    def _():
        acc_ref[...] = jnp.zeros_like(acc_ref)

    @pl.when(i == 0)
    def _():
        if seed_ref is not None:
            finish_slice(seed_ref[...])
        accumulate(add)

    @pl.when((i > 0) & (i < last))
    def _():
        finish_from_acc()
        accumulate(add)

    @pl.when(i == last)
    def _():
        finish_from_acc()


def _lagged_tile_maps(n_tiles, n_steps, seeded):
    first = 1 if seeded else 0
    rows = n_tiles - first + 1
    tile_in = lambda i, j: jnp.minimum(i + first, n_tiles - 1)
    slice_out = lambda i, j: jnp.where(i + first == 0, 0, (i + first - 1) * n_steps + j)
    step = lambda i, j: jnp.where(i < rows - 1, j, n_steps - 1)
    return rows, tile_in, slice_out, step


def _out_proj_kernel(y_ref, w_ref, h_ref, gpost_ref, gnext_ref, o_ref, un_ref, acc_ref, *, n_k):
    tk = y_ref.shape[1]

    def accumulate(add):
        w_rows = pl.ds(pl.multiple_of(pl.program_id(1) * tk, tk), tk)
        add(jnp.dot(y_ref[...], w_ref[w_rows, :], preferred_element_type=F32))

    _lagged_residual_steps(accumulate, acc_ref, None, h_ref, gpost_ref, gnext_ref, o_ref, un_ref,
                           n_k)


def out_proj_residual(y, w, w_layer, h, g_post, layer, g_next, next_layer, *, name):
    t, k = y.shape
    d = w.shape[2]
    tm, tk = OUT_TM, OUT_TK
    n, n_k = t // tm, k // tk
    rows, tile_in, slice_out, step = _lagged_tile_maps(n, n_k, seeded=False)
    finished = pl.BlockSpec((tm // n_k, d), lambda i, j: (slice_out(i, j), 0))
    return pl.pallas_call(
        functools.partial(_out_proj_kernel, n_k=n_k),
        grid=(rows, n_k),
        in_specs=[pl.BlockSpec((tm, tk), lambda i, j: (tile_in(i, j), step(i, j))),
                  pl.BlockSpec((None, k, d), lambda i, j: (w_layer, 0, 0),
                               pipeline_mode=pl.Buffered(1)),
                  finished,
                  _gain_spec(d, layer),
                  _gain_spec(d, next_layer)],
        out_specs=[finished, finished],
        out_shape=[jax.ShapeDtypeStruct((t, d), F32), jax.ShapeDtypeStruct((t, d), BF16)],
        scratch_shapes=[pltpu.VMEM((2, tm, d), F32)],
        compiler_params=_params(("arbitrary", "arbitrary")),
        name=name,
    )(y, w, h, _gains(g_post), _gains(g_next))


def _mlp_product(u, wup, wdn):
    hid = jnp.maximum(jnp.dot(u, wup, preferred_element_type=F32), 0.0)
    return jnp.dot((hid * hid).astype(BF16), wdn, preferred_element_type=F32)


def _mlp_first_tile_kernel(u_ref, wup_ref, wdn_ref, acc_ref, wupb_ref, wdnb_ref):
    wup = wup_ref[...].astype(BF16)
    wdn = wdn_ref[...].astype(BF16)
    wupb_ref[...] = wup
    wdnb_ref[...] = wdn

    @pl.when(pl.program_id(0) == 0)
    def _():
        acc_ref[...] = jnp.zeros_like(acc_ref)

    acc_ref[...] += _mlp_product(u_ref[...], wup, wdn)


def mlp_first_tile(u, wup, wdn, layer, *, name):
    t, d = u.shape
    f = wup.shape[2]
    tm, tf = MLP_TM, MLP_FIRST_TF
    return pl.pallas_call(
        _mlp_first_tile_kernel,
        grid=(f // tf,),
        in_specs=[pl.BlockSpec((tm, d), lambda j: (0, 0)),
                  pl.BlockSpec((None, d, tf), lambda j: (layer, 0, j)),
                  pl.BlockSpec((None, tf, d), lambda j: (layer, j, 0))],
        out_specs=[pl.BlockSpec((tm, d), lambda j: (0, 0)),
                   pl.BlockSpec((d, tf), lambda j: (0, j)),
                   pl.BlockSpec((tf, d), lambda j: (j, 0))],
        out_shape=[jax.ShapeDtypeStruct((tm, d), F32),
                   jax.ShapeDtypeStruct((d, f), BF16),
                   jax.ShapeDtypeStruct((f, d), BF16)],
        compiler_params=_params(("arbitrary",)),
        name=name,
    )(u, wup, wdn)


def _mlp_kernel(u_ref, seed_ref, h_ref, wup_ref, wdn_ref, gpost_ref, *rest, n_f, emit_u):
    if emit_u:
        gnext_ref, o_ref, un_ref, acc_ref = rest
    else:
        (o_ref, acc_ref), gnext_ref, un_ref = rest, None, None

    def accumulate(add):
        add(_mlp_product(u_ref[...], wup_ref[...], wdn_ref[...]))

    _lagged_residual_steps(accumulate, acc_ref, seed_ref, h_ref, gpost_ref, gnext_ref, o_ref,
                           un_ref, n_f)


def mlp_residual(u, h, wup, wdn, gpost, layer, g_next=None, next_layer=None, *, name):
    t, d = h.shape
    f = wup.shape[2]
    tm, tf = MLP_TM, MLP_TF
    n, n_f = t // tm, f // tf
    emit_u = g_next is not None
    seed, wup, wdn = mlp_first_tile(u, wup, wdn, layer, name=name + "_first")
    rows, tile_in, slice_out, step = _lagged_tile_maps(n, n_f, seeded=True)
    finished = pl.BlockSpec((tm // n_f, d), lambda i, j: (slice_out(i, j), 0))
    in_specs = [pl.BlockSpec((tm, d), lambda i, j: (tile_in(i, j), 0)),
                pl.BlockSpec((tm // n_f, d), lambda i, j: (jnp.where(i == 0, j, n_f - 1), 0)),
                finished,
                pl.BlockSpec((d, tf), lambda i, j: (0, step(i, j))),
                pl.BlockSpec((tf, d), lambda i, j: (step(i, j), 0)),
                _gain_spec(d, layer)]
    args = [u, seed, h, wup, wdn, _gains(gpost)]
    out_specs = [finished]
    out_shape = [jax.ShapeDtypeStruct((t, d), F32)]
    if emit_u:
        in_specs.append(_gain_spec(d, next_layer))
        args.append(_gains(g_next))
        out_specs.append(finished)
        out_shape.append(jax.ShapeDtypeStruct((t, d), BF16))
    return pl.pallas_call(
        functools.partial(_mlp_kernel, n_f=n_f, emit_u=emit_u),
        grid=(rows, n_f),
        in_specs=in_specs,
        out_specs=out_specs,
        out_shape=out_shape,
        scratch_shapes=[pltpu.VMEM((2, tm, d), F32)],
        compiler_params=_params(("arbitrary", "arbitrary")),
        name=name,
    )(*args)


def _retention_kernel(tab_ref, q_ref, k_ref, v_ref, g_ref, cos_ref, sin_ref, gnw_ref,
                      o_ref, state_ref, dmask_ref, xi_ref, zeta_ref, *, n_heads, blk):
    head = pl.program_id(2)
    half = RET_HEAD_DK // 2
    log_gamma = tab_ref[head]
    gamma_blk = tab_ref[n_heads + head]

    @pl.when(pl.program_id(1) == 0)
    def _():
        state_ref[head] = jnp.zeros(state_ref.shape[1:], F32)
        ri = lax.broadcasted_iota(jnp.int32, (blk, blk), 0)
        ci = lax.broadcasted_iota(jnp.int32, (blk, blk), 1)
        dist = jnp.abs(ri - ci).astype(F32)
        shift = int(math.log2(CHUNK))
        visible = jnp.right_shift(ci, shift) <= jnp.right_shift(ri, shift)
        dmask_ref[head] = jnp.where(visible, jnp.exp(dist * log_gamma), 0.0)
        pos = lax.broadcasted_iota(jnp.int32, (blk, V7X_LANES), 0).astype(F32)
        xi_ref[head] = jnp.exp((pos + 1.0) * log_gamma)
        zeta_ref[head] = jnp.exp((blk - 1.0 - pos) * log_gamma)

    xi = jnp.concatenate([xi_ref[head]] * (RET_HEAD_DV // V7X_LANES), axis=1)
    zeta = jnp.concatenate([zeta_ref[head]] * (RET_HEAD_DK // V7X_LANES), axis=1)
    dmask = dmask_ref[head]
    gnw = gnw_ref[...]

    for sub in range(q_ref.shape[0] // blk):
        rows = pl.ds(sub * blk, blk)
        cos = cos_ref[rows, :]
        sin = sin_ref[rows, :]

        def rotary(t):
            t1, t2 = t[:, :half], t[:, half:]
            return jnp.concatenate([t1 * cos - t2 * sin, t1 * sin + t2 * cos], axis=1)

        qr = rotary(q_ref[rows, :].astype(F32))
        kr = rotary(k_ref[rows, :].astype(F32)) * (RET_HEAD_DK ** -0.5)
        qb = qr.astype(BF16)
        kb = kr.astype(BF16)
        vb = v_ref[rows, :].astype(BF16)

        scores = lax.dot_general(qb, kb, (((1,), (1,)), ((), ())), preferred_element_type=F32)
        inner = jnp.dot((scores * dmask).astype(BF16), vb, preferred_element_type=F32)
        state = state_ref[head]
        cross = jnp.dot(qb, state.astype(BF16), preferred_element_type=F32)
        o = inner + cross * xi

        kz = (kr * zeta).astype(BF16)
        upd = lax.dot_general(kz, vb, (((0,), (0,)), ((), ())), preferred_element_type=F32)
        state_ref[head] = state * gamma_blk + upd

        mu = jnp.mean(o, axis=-1, keepdims=True)
        dev = o - mu
        var = jnp.mean(dev * dev, axis=-1, keepdims=True)
        normed = dev * lax.rsqrt(var + GN_EPS) * gnw
        o_ref[rows, :] = (_silu(g_ref[rows, :].astype(F32)) * normed).astype(o_ref.dtype)


def _rotary_tables(seq, half, blk):
    inv_freq = ROPE_BASE ** (-jnp.arange(half, dtype=F32) / half)
    ang_blk = (jnp.arange(seq // blk) * blk).astype(F32)[:, None] * inv_freq[None, :]
    ang_in = jnp.arange(blk).astype(F32)[:, None] * inv_freq[None, :]
    cb, sb = jnp.cos(ang_blk)[:, None, :], jnp.sin(ang_blk)[:, None, :]
    ci, si = jnp.cos(ang_in)[None], jnp.sin(ang_in)[None]
    return (cb * ci - sb * si).reshape(seq, half), (sb * ci + cb * si).reshape(seq, half)


def retention_core(proj, gn_w, *, batch, seq, name):
    t = proj.shape[0]
    n_heads = proj.shape[1] // (2 * RET_HEAD_DK + 2 * RET_HEAD_DV)
    step, blk = RET_STEP, RET_BLK
    steps = seq // step
    dv_per_dk = RET_HEAD_DV // RET_HEAD_DK
    k_off = n_heads
    v_off = 2 * n_heads // dv_per_dk
    g_off = v_off + n_heads

    half = RET_HEAD_DK // 2
    cos, sin = _rotary_tables(seq, half, blk)
    log_gamma = jnp.log1p(-jnp.exp2(-5.0 - jnp.arange(n_heads, dtype=F32)))
    tab = jnp.concatenate([log_gamma, jnp.exp(blk * log_gamma)])

    row = lambda b, c, h: b * steps + c
    return pl.pallas_call(
        functools.partial(_retention_kernel, n_heads=n_heads, blk=blk),
        grid=(batch, steps, n_heads),
        in_specs=[pl.BlockSpec(memory_space=pltpu.SMEM),
                  pl.BlockSpec((step, RET_HEAD_DK), lambda b, c, h: (row(b, c, h), h)),
                  pl.BlockSpec((step, RET_HEAD_DK), lambda b, c, h: (row(b, c, h), k_off + h)),
                  pl.BlockSpec((step, RET_HEAD_DV), lambda b, c, h: (row(b, c, h), v_off + h)),
                  pl.BlockSpec((step, RET_HEAD_DV), lambda b, c, h: (row(b, c, h), g_off + h)),
                  pl.BlockSpec((step, half), lambda b, c, h: (c, 0)),
                  pl.BlockSpec((step, half), lambda b, c, h: (c, 0)),
                  pl.BlockSpec((1, RET_HEAD_DV), lambda b, c, h: (0, h))],
        out_specs=pl.BlockSpec((step, RET_HEAD_DV), lambda b, c, h: (row(b, c, h), h)),
        out_shape=jax.ShapeDtypeStruct((t, n_heads * RET_HEAD_DV), BF16),
        scratch_shapes=[pltpu.VMEM((n_heads, RET_HEAD_DK, RET_HEAD_DV), F32),
                        pltpu.VMEM((n_heads, blk, blk), F32),
                        pltpu.VMEM((n_heads, blk, V7X_LANES), F32),
                        pltpu.VMEM((n_heads, blk, V7X_LANES), F32)],
        compiler_params=_params(("parallel", "arbitrary", "arbitrary")),
        name=name,
    )(tab, proj, proj, proj, proj, cos, sin, gn_w.reshape(1, -1))


def _softplus(x):
    return jnp.maximum(x, 0.0) + jnp.log1p(jnp.exp(-jnp.abs(x)))


def _split_bf16(v, terms):
    parts = []
    for _ in range(terms - 1):
        part = v.astype(BF16)
        parts.append(part)
        v = v - part.astype(F32)
    parts.append(v.astype(BF16))
    return jnp.concatenate(parts, axis=0)


def _sum_terms(stacked, rows):
    out = stacked[0:rows]
    for i in range(1, stacked.shape[0] // rows):
        out = out + stacked[i * rows:(i + 1) * rows]
    return out


def _ssd_decay_kernel(dt_ref, prm_ref, src_ref, acum_ref, inw_ref, *, blk):
    prm = prm_ref[...]
    a2 = -jnp.exp(prm[:, 1:2]) * LOG2_E
    ri = lax.broadcasted_iota(jnp.int32, (blk, blk), 0)
    ci = lax.broadcasted_iota(jnp.int32, (blk, blk), 1)
    upper_ones = jnp.where(ri <= ci, 1.0, 0.0).astype(BF16)
    n_heads = dt_ref.shape[0]
    for sub in range(dt_ref.shape[1] // blk):
        cols = pl.ds(sub * blk, blk)
        dt = _softplus(dt_ref[:, cols] + prm[:, 0:1])
        acum = _sum_terms(jnp.dot(_split_bf16(dt * a2, 3), upper_ones,
                                  preferred_element_type=F32), n_heads)
        src_ref[:, cols] = acum - jnp.log2(dt)
        acum_ref[:, cols] = acum
        inw_ref[:, cols] = dt * jnp.exp2(acum[:, blk - 1:blk] - acum)


def ssd_decays(dt_raw_t, dt_bias, a_log, *, blk, name):
    n_heads, t = dt_raw_t.shape
    step = SSD_DECAY_STEP
    spec = pl.BlockSpec((n_heads, step), lambda i: (0, i))
    shape = jax.ShapeDtypeStruct((n_heads, t), F32)
    return pl.pallas_call(
        functools.partial(_ssd_decay_kernel, blk=blk),
        grid=(t // step,),
        in_specs=[spec, pl.BlockSpec((n_heads, 2), lambda i: (0, 0))],
        out_specs=[spec, spec, spec],
        out_shape=[shape, shape, shape],
        compiler_params=_params(("parallel",)),
        name=name,
    )(dt_raw_t, jnp.stack([dt_bias, a_log], axis=-1))


def _ssd_kernel(z_ref, x_ref, b_ref, c_ref, wx_ref, wb_ref, wc_ref, bx_ref, bb_ref, bc_ref,
                src_ref, acum_ref, inw_ref, dexp_ref, nw_ref, o_ref,
                state_ref, xe_ref, be_ref, ce_ref, *, blk):
    step_rows = x_ref.shape[0]
    halo = V7X_SUBLANES

    @pl.when(pl.program_id(2) == 0)
    def _():
        state_ref[...] = jnp.zeros_like(state_ref)
        xe_ref[0:halo, :] = jnp.zeros((halo, xe_ref.shape[1]), F32)
        be_ref[0:halo, :] = jnp.zeros((halo, be_ref.shape[1]), F32)
        ce_ref[0:halo, :] = jnp.zeros((halo, ce_ref.shape[1]), F32)

    xe_ref[halo:, :] = x_ref[...].astype(F32)
    be_ref[halo:, :] = b_ref[...].astype(F32)
    ce_ref[halo:, :] = c_ref[...].astype(F32)

    def conv_silu(ext_ref, w_ref, bias_ref, off):
        out = bias_ref[...] + ext_ref[pl.ds(halo + off, blk), :] * w_ref[SSD_CONV_W - 1:SSD_CONV_W, :]
        for s in range(1, SSD_CONV_W):
            out = out + (ext_ref[pl.ds(halo + off - s, blk), :]
                         * w_ref[SSD_CONV_W - 1 - s:SSD_CONV_W - s, :])
        return _silu(out)

    ri = lax.broadcasted_iota(jnp.int32, (blk, blk), 0)
    ci = lax.broadcasted_iota(jnp.int32, (blk, blk), 1)
    causal = ri >= ci
    lane = lax.broadcasted_iota(jnp.int32, (blk, V7X_LANES), 1)
    first = lane < SSD_HEADDIM
    keep_first = jnp.where(first, 1.0, 0.0).astype(BF16)
    keep_second = jnp.where(first, 0.0, 1.0).astype(BF16)
    dexp = dexp_ref[...]
    nw = nw_ref[...]
    n_heads = acum_ref.shape[0]
    gw = x_ref.shape[1]

    def head_selector(terms, lanes_per_head):
        k = lax.broadcasted_iota(jnp.int32, (terms * n_heads, n_heads * lanes_per_head), 0)
        n = lax.broadcasted_iota(jnp.int32, (terms * n_heads, n_heads * lanes_per_head), 1)
        shift = int(math.log2(lanes_per_head))
        return jnp.where((k & (n_heads - 1)) == jnp.right_shift(n, shift), 1.0, 0.0).astype(BF16)

    sel_mask = head_selector(3, blk)
    sel_chan = head_selector(2, SSD_HEADDIM)
    transposed_lhs = (((0,), (0,)), ((), ()))

    for sub in range(step_rows // blk):
        off = sub * blk
        src_r = src_ref[:, pl.ds(off, blk)]
        acum_r = acum_ref[:, pl.ds(off, blk)]
        acum_terms = _split_bf16(acum_r, 3)
        in_weight = lax.dot_general(_split_bf16(inw_ref[:, pl.ds(off, blk)], 2), sel_chan,
                                    transposed_lhs, preferred_element_type=F32)
        xs = conv_silu(xe_ref, wx_ref, bx_ref, off)
        bmb = conv_silu(be_ref, wb_ref, bb_ref, off).astype(BF16)
        cmb = conv_silu(ce_ref, wc_ref, bc_ref, off).astype(BF16)

        cb = lax.dot_general(cmb, bmb, (((1,), (1,)), ((), ())), preferred_element_type=F32)
        state = state_ref[...]
        y_off = jnp.dot(cmb, state.astype(BF16), preferred_element_type=F32)

        y_tiles, decay_tiles = [], []
        for pair in range(gw // V7X_LANES):
            r0, r1 = 2 * pair, 2 * pair + 1
            cols = slice(pair * V7X_LANES, (pair + 1) * V7X_LANES)
            xp = xs[:, cols]
            xpb = xp.astype(BF16)
            acum_bc = lax.dot_general(acum_terms, sel_mask[:, r0 * blk:(r1 + 1) * blk],
                                      transposed_lhs, preferred_element_type=F32)
            out_decay = jnp.exp2(jnp.where(first, acum_bc[:, 0:V7X_LANES],
                                           acum_bc[:, blk:blk + V7X_LANES]))
            y = y_off[:, cols] * out_decay + dexp[:, cols] * xp
            for k, (r, keep) in enumerate(((r0, keep_first), (r1, keep_second))):
                seg = acum_bc[:, k * blk:(k + 1) * blk] - src_r[r:r + 1, :]
                lmat = jnp.exp2(jnp.where(causal, seg, -jnp.inf))
                y = y + jnp.dot((cb * lmat).astype(BF16), xpb * keep,
                                preferred_element_type=F32)
            y_tiles.append(y)
            decay_tiles.append(out_decay[blk - 1:blk, :])
        y = jnp.concatenate(y_tiles, axis=1)
        xw = (xs * in_weight).astype(BF16)
        upd = lax.dot_general(bmb, xw, transposed_lhs, preferred_element_type=F32)
        state_ref[...] = state * jnp.concatenate(decay_tiles, axis=1) + upd

        yg = y * _silu(z_ref[pl.ds(off, blk), :].astype(F32))
        o_ref[pl.ds(off, blk), :] = (yg * _rms_scale(yg) * nw).astype(o_ref.dtype)

    xe_ref[0:halo, :] = xe_ref[step_rows:step_rows + halo, :]
    be_ref[0:halo, :] = be_ref[step_rows:step_rows + halo, :]
    ce_ref[0:halo, :] = ce_ref[step_rows:step_rows + halo, :]


def ssd_core(zxbc, dt_raw_t, conv_w, conv_b, dt_bias, a_log, d_skip, norm_w, *, batch, seq, name):
    t = zxbc.shape[0]
    g, r, n, gw = SSD_GROUPS, SSD_HEADS_PER_GROUP, SSD_STATE, SSD_GROUP_WIDTH
    d_inner = g * gw
    step, blk = SSD_STEP, SSD_BLK
    steps = seq // step
    x_off = d_inner // gw
    b_off = 2 * d_inner // n
    c_off = b_off + g
    wb_off = d_inner // n
    wc_off = wb_off + g

    src_r, acum_r, inw_r = (a.reshape(g, r, t) for a in
                           ssd_decays(dt_raw_t, dt_bias, a_log, blk=blk, name=name + "_decays"))
    dexp = jnp.repeat(d_skip, SSD_HEADDIM).reshape(1, d_inner)
    conv_b = conv_b.reshape(1, -1)

    row = lambda b, gi, c: b * steps + c
    return pl.pallas_call(
        functools.partial(_ssd_kernel, blk=blk),
        grid=(batch, g, steps),
        in_specs=[pl.BlockSpec((step, gw), lambda b, gi, c: (row(b, gi, c), gi)),
                  pl.BlockSpec((step, gw), lambda b, gi, c: (row(b, gi, c), x_off + gi)),
                  pl.BlockSpec((step, n), lambda b, gi, c: (row(b, gi, c), b_off + gi)),
                  pl.BlockSpec((step, n), lambda b, gi, c: (row(b, gi, c), c_off + gi)),
                  pl.BlockSpec((SSD_CONV_W, gw), lambda b, gi, c: (0, gi)),
                  pl.BlockSpec((SSD_CONV_W, n), lambda b, gi, c: (0, wb_off + gi)),
                  pl.BlockSpec((SSD_CONV_W, n), lambda b, gi, c: (0, wc_off + gi)),
                  pl.BlockSpec((1, gw), lambda b, gi, c: (0, gi)),
                  pl.BlockSpec((1, n), lambda b, gi, c: (0, wb_off + gi)),
                  pl.BlockSpec((1, n), lambda b, gi, c: (0, wc_off + gi)),
                  pl.BlockSpec((None, r, step), lambda b, gi, c: (gi, 0, row(b, gi, c))),
                  pl.BlockSpec((None, r, step), lambda b, gi, c: (gi, 0, row(b, gi, c))),
                  pl.BlockSpec((None, r, step), lambda b, gi, c: (gi, 0, row(b, gi, c))),
                  pl.BlockSpec((1, gw), lambda b, gi, c: (0, gi)),
                  pl.BlockSpec((1, gw), lambda b, gi, c: (0, gi))],
        out_specs=pl.BlockSpec((step, gw), lambda b, gi, c: (row(b, gi, c), gi)),
        out_shape=jax.ShapeDtypeStruct((t, d_inner), BF16),
        scratch_shapes=[pltpu.VMEM((n, gw), F32),
                        pltpu.VMEM((step + V7X_SUBLANES, gw), F32),
                        pltpu.VMEM((step + V7X_SUBLANES, n), F32),
                        pltpu.VMEM((step + V7X_SUBLANES, n), F32)],
        compiler_params=_params(("parallel", "parallel", "arbitrary")),
        name=name,
    )(zxbc, zxbc, zxbc, zxbc, conv_w, conv_w, conv_w, conv_b, conv_b, conv_b,
      src_r, acum_r, inw_r, dexp, norm_w.reshape(1, d_inner))


def kernel(x, norm_mix_pre, norm_mix_post, norm_ffn_pre, norm_ffn_post, ret_w_in, ret_gn_w, ret_w_out, ssd_w_in, ssd_conv_w, ssd_conv_b, ssd_dt_bias, ssd_a_log, ssd_d, ssd_norm_w, ssd_w_out, mlp_w_up, mlp_w_down):
    batch, seq, d = x.shape
    h = x.reshape(batch * seq, d)

    n_zxbc = ssd_w_in.shape[2] - SSD_GROUPS * SSD_HEADS_PER_GROUP
    ssd_w_in_t = jnp.swapaxes(ssd_w_in, 1, 2)
    ret_w_in, ret_w_out, ssd_w_out = (w.astype(BF16) for w in (ret_w_in, ret_w_out, ssd_w_out))

    proj = norm_matmul(h, norm_mix_pre, 0, ret_w_in, 0, out_dtype=BF16, name="ret_in_proj")
    y = retention_core(proj, ret_gn_w[0], batch=batch, seq=seq, name="retention_core")
    h, u = out_proj_residual(y, ret_w_out, 0, h, norm_mix_post, 0, norm_ffn_pre, 0,
                             name="ret_out_proj")
    h, u = mlp_residual(u, h, mlp_w_up, mlp_w_down, norm_ffn_post, 0, norm_mix_pre, 1, name="mlp0")

    zxbc, dt_raw_t = matmul_dt(u, ssd_w_in_t, 0, n_out=n_zxbc, out_dtype=F32,
                               name="ssd_in_proj")
    y = ssd_core(zxbc, dt_raw_t, ssd_conv_w[0], ssd_conv_b[0], ssd_dt_bias[0], ssd_a_log[0],
                 ssd_d[0], ssd_norm_w[0], batch=batch, seq=seq, name="ssd_core")
    h, u = out_proj_residual(y, ssd_w_out, 0, h, norm_mix_post, 1, norm_ffn_pre, 1,
                             name="ssd_out_proj")
    (h,) = mlp_residual(u, h, mlp_w_up, mlp_w_down, norm_ffn_post, 1, name="mlp1")
    return h.reshape(batch, seq, d)
```

```python
import functools
import math

import jax
import jax.numpy as jnp
from jax import lax
from jax.experimental import pallas as pl
from jax.experimental.pallas import tpu as pltpu

F32 = jnp.float32
BF16 = jnp.bfloat16

RMS_EPS = 1e-6
GN_EPS = 1e-5
ROPE_BASE = 10000.0
CHUNK = 64
LOG2_E = math.log2(math.e)

RET_HEAD_DK = 256
RET_HEAD_DV = 512
SSD_HEADDIM = 64
SSD_GROUPS = 8
SSD_HEADS_PER_GROUP = 8
SSD_STATE = 128
SSD_CONV_W = 4
SSD_GROUP_WIDTH = SSD_HEADS_PER_GROUP * SSD_HEADDIM

V7X_LANES = 128
V7X_SUBLANES = 8
V7X_VMEM_LIMIT_BYTES = 56 * 1024 * 1024

PROJ_TM, PROJ_TN = 1024, 2048
PROJ_WCAST_TN = 1024
OUT_TM, OUT_TK = 512, 2048
MLP_TM, MLP_TF = 1024, 1024
MLP_FIRST_TF = 512
RET_STEP, RET_BLK = 2048, 256
SSD_DECAY_STEP = 2048
SSD_STEP, SSD_BLK = 2048, 128


def _params(semantics):
    return pltpu.CompilerParams(dimension_semantics=semantics,
                                vmem_limit_bytes=V7X_VMEM_LIMIT_BYTES)


def _rms_scale(x):
    return lax.rsqrt(jnp.mean(x * x, axis=-1, keepdims=True) + RMS_EPS)


def _silu(x):
    half = 0.5 * x
    return half + half * jnp.tanh(half)


def _norm_matmul_kernel(x_ref, g_ref, w_ref, o_ref, u_ref):
    @pl.when(pl.program_id(1) == 0)
    def _():
        x = x_ref[...]
        u_ref[...] = (x * _rms_scale(x) * g_ref[...]).astype(BF16)

    o_ref[...] = jnp.dot(u_ref[...], w_ref[...],
                         preferred_element_type=F32).astype(o_ref.dtype)


def _matmul_dt_kernel(u_ref, wt_ref, wdt_ref, o_ref, dt_ref, wb_ref):
    contract_last = (((1,), (1,)), ((), ()))

    @pl.when(pl.program_id(1) == 0)
    def _():
        wb_ref[...] = wt_ref[...].astype(BF16)

    @pl.when(pl.program_id(0) == 0)
    def _():
        dt_ref[...] = lax.dot_general(wdt_ref[...].astype(BF16), u_ref[...], contract_last,
                                      preferred_element_type=F32)

    o_ref[...] = lax.dot_general(u_ref[...], wb_ref[...], contract_last,
                                 preferred_element_type=F32).astype(o_ref.dtype)


def _gain_spec(d, layer):
    return pl.BlockSpec((None, 1, d), lambda *_: (layer, 0, 0))


def _gains(g):
    return g.reshape(g.shape[0], 1, g.shape[1])


def norm_matmul(h, g, layer, w, w_layer, *, out_dtype, name):
    t, d = h.shape
    n = w.shape[2]
    tm, tn = PROJ_TM, PROJ_TN
    return pl.pallas_call(
        _norm_matmul_kernel,
        grid=(t // tm, n // tn),
        in_specs=[pl.BlockSpec((tm, d), lambda i, j: (i, 0)),
                  _gain_spec(d, layer),
                  pl.BlockSpec((None, d, tn), lambda i, j: (w_layer, 0, j))],
        out_specs=pl.BlockSpec((tm, tn), lambda i, j: (i, j)),
        out_shape=jax.ShapeDtypeStruct((t, n), out_dtype),
        scratch_shapes=[pltpu.VMEM((tm, d), BF16)],
        compiler_params=_params(("parallel", "arbitrary")),
        name=name,
    )(h, _gains(g), w)


def matmul_dt(u, w_t, w_layer, *, n_out, out_dtype, name):
    t, d = u.shape
    n_dt = w_t.shape[1] - n_out
    tm, tn = PROJ_TM, PROJ_WCAST_TN
    n_rows = t // tm
    dt_rows = n_out // n_dt
    dt_block = lambda j, i: (0, jnp.where(j == 0, i, n_rows - 1))
    return pl.pallas_call(
        _matmul_dt_kernel,
        grid=(n_out // tn, n_rows),
        in_specs=[pl.BlockSpec((tm, d), lambda j, i: (i, 0)),
                  pl.BlockSpec((None, tn, d), lambda j, i: (w_layer, j, 0)),
                  pl.BlockSpec((None, n_dt, d), lambda j, i: (w_layer, dt_rows, 0))],
        out_specs=[pl.BlockSpec((tm, tn), lambda j, i: (i, j)),
                   pl.BlockSpec((n_dt, tm), dt_block)],
        out_shape=[jax.ShapeDtypeStruct((t, n_out), out_dtype),
                   jax.ShapeDtypeStruct((n_dt, t), F32)],
        scratch_shapes=[pltpu.VMEM((tn, d), BF16)],
        compiler_params=_params(("arbitrary", "arbitrary")),
        name=name,
    )(u, w_t, w_t)


def _residual_norm(h, m, g_post, g_next=None):
    hn = h + m * _rms_scale(m) * g_post
    if g_next is None:
        return hn, None
    return hn, (hn * _rms_scale(hn) * g_next).astype(BF16)


def _lagged_residual_steps(accumulate, acc_ref, seed_ref, h_ref, gpost_ref, gnext_ref, o_ref,
                           un_ref, n_slices):
    i = pl.program_id(0)
    j = pl.program_id(1)
    last = pl.num_programs(0) - 1
    first = 0 if seed_ref is None else 1
    slice_rows = acc_ref.shape[1] // n_slices

    def add(part):
        slot = (i + first) % 2
        acc_ref[slot] = jnp.where(j == 0, part, acc_ref[slot] + part)

    def finish_slice(m):
        hn, un = _residual_norm(h_ref[...], m, gpost_ref[...],
                                None if gnext_ref is None else gnext_ref[...])
        o_ref[...] = hn
        if un_ref is not None:
            un_ref[...] = un

    def finish_from_acc():
        rows = pl.ds(pl.multiple_of(j * slice_rows, slice_rows), slice_rows)
        finish_slice(acc_ref[(i + first + 1) % 2, rows, :])

    @pl.when((i == 0) & (j == 0))
    def _():
        acc_ref[...] = jnp.zeros_like(acc_ref)

    @pl.when(i == 0)
    def _():
        if seed_ref is not None:
            finish_slice(seed_ref[...])
        accumulate(add)

    @pl.when((i > 0) & (i < last))
    def _():
        finish_from_acc()
        accumulate(add)

    @pl.when(i == last)
    def _():
        finish_from_acc()


def _lagged_tile_maps(n_tiles, n_steps, seeded):
    first = 1 if seeded else 0
    rows = n_tiles - first + 1
    tile_in = lambda i, j: jnp.minimum(i + first, n_tiles - 1)
    slice_out = lambda i, j: jnp.where(i + first == 0, 0, (i + first - 1) * n_steps + j)
    step = lambda i, j: jnp.where(i < rows - 1, j, n_steps - 1)
    return rows, tile_in, slice_out, step


def _out_proj_kernel(y_ref, w_ref, h_ref, gpost_ref, gnext_ref, o_ref, un_ref, acc_ref, *, n_k):
    tk = y_ref.shape[1]

    def accumulate(add):
        w_rows = pl.ds(pl.multiple_of(pl.program_id(1) * tk, tk), tk)
        add(jnp.dot(y_ref[...], w_ref[w_rows, :], preferred_element_type=F32))

    _lagged_residual_steps(accumulate, acc_ref, None, h_ref, gpost_ref, gnext_ref, o_ref, un_ref,
                           n_k)


def out_proj_residual(y, w, w_layer, h, g_post, layer, g_next, next_layer, *, name):
    t, k = y.shape
    d = w.shape[2]
    tm, tk = OUT_TM, OUT_TK
    n, n_k = t // tm, k // tk
    rows, tile_in, slice_out, step = _lagged_tile_maps(n, n_k, seeded=False)
    finished = pl.BlockSpec((tm // n_k, d), lambda i, j: (slice_out(i, j), 0))
    return pl.pallas_call(
        functools.partial(_out_proj_kernel, n_k=n_k),
        grid=(rows, n_k),
        in_specs=[pl.BlockSpec((tm, tk), lambda i, j: (tile_in(i, j), step(i, j))),
                  pl.BlockSpec((None, k, d), lambda i, j: (w_layer, 0, 0),
                               pipeline_mode=pl.Buffered(1)),
                  finished,
                  _gain_spec(d, layer),
                  _gain_spec(d, next_layer)],
        out_specs=[finished, finished],
        out_shape=[jax.ShapeDtypeStruct((t, d), F32), jax.ShapeDtypeStruct((t, d), BF16)],
        scratch_shapes=[pltpu.VMEM((2, tm, d), F32)],
        compiler_params=_params(("arbitrary", "arbitrary")),
        name=name,
    )(y, w, h, _gains(g_post), _gains(g_next))


def _mlp_product(u, wup, wdn):
    hid = jnp.maximum(jnp.dot(u, wup, preferred_element_type=F32), 0.0)
    return jnp.dot((hid * hid).astype(BF16), wdn, preferred_element_type=F32)


def _mlp_first_tile_kernel(u_ref, wup_ref, wdn_ref, acc_ref, wupb_ref, wdnb_ref):
    wup = wup_ref[...].astype(BF16)
    wdn = wdn_ref[...].astype(BF16)
    wupb_ref[...] = wup
    wdnb_ref[...] = wdn

    @pl.when(pl.program_id(0) == 0)
    def _():
        acc_ref[...] = jnp.zeros_like(acc_ref)

    acc_ref[...] += _mlp_product(u_ref[...], wup, wdn)


def mlp_first_tile(u, wup, wdn, layer, *, name):
    t, d = u.shape
    f = wup.shape[2]
    tm, tf = MLP_TM, MLP_FIRST_TF
    return pl.pallas_call(
        _mlp_first_tile_kernel,
        grid=(f // tf,),
        in_specs=[pl.BlockSpec((tm, d), lambda j: (0, 0)),
                  pl.BlockSpec((None, d, tf), lambda j: (layer, 0, j)),
                  pl.BlockSpec((None, tf, d), lambda j: (layer, j, 0))],
        out_specs=[pl.BlockSpec((tm, d), lambda j: (0, 0)),
                   pl.BlockSpec((d, tf), lambda j: (0, j)),
                   pl.BlockSpec((tf, d), lambda j: (j, 0))],
        out_shape=[jax.ShapeDtypeStruct((tm, d), F32),
                   jax.ShapeDtypeStruct((d, f), BF16),
                   jax.ShapeDtypeStruct((f, d), BF16)],
        compiler_params=_params(("arbitrary",)),
        name=name,
    )(u, wup, wdn)


def _mlp_kernel(u_ref, seed_ref, h_ref, wup_ref, wdn_ref, gpost_ref, *rest, n_f, emit_u):
    if emit_u:
        gnext_ref, o_ref, un_ref, acc_ref = rest
    else:
        (o_ref, acc_ref), gnext_ref, un_ref = rest, None, None

    def accumulate(add):
        add(_mlp_product(u_ref[...], wup_ref[...], wdn_ref[...]))

    _lagged_residual_steps(accumulate, acc_ref, seed_ref, h_ref, gpost_ref, gnext_ref, o_ref,
                           un_ref, n_f)


def mlp_residual(u, h, wup, wdn, gpost, layer, g_next=None, next_layer=None, *, name):
    t, d = h.shape
    f = wup.shape[2]
    tm, tf = MLP_TM, MLP_TF
    n, n_f = t // tm, f // tf
    emit_u = g_next is not None
    seed, wup, wdn = mlp_first_tile(u, wup, wdn, layer, name=name + "_first")
    rows, tile_in, slice_out, step = _lagged_tile_maps(n, n_f, seeded=True)
    finished = pl.BlockSpec((tm // n_f, d), lambda i, j: (slice_out(i, j), 0))
    in_specs = [pl.BlockSpec((tm, d), lambda i, j: (tile_in(i, j), 0)),
                pl.BlockSpec((tm // n_f, d), lambda i, j: (jnp.where(i == 0, j, n_f - 1), 0)),
                finished,
                pl.BlockSpec((d, tf), lambda i, j: (0, step(i, j))),
                pl.BlockSpec((tf, d), lambda i, j: (step(i, j), 0)),
                _gain_spec(d, layer)]
    args = [u, seed, h, wup, wdn, _gains(gpost)]
    out_specs = [finished]
    out_shape = [jax.ShapeDtypeStruct((t, d), F32)]
    if emit_u:
        in_specs.append(_gain_spec(d, next_layer))
        args.append(_gains(g_next))
        out_specs.append(finished)
        out_shape.append(jax.ShapeDtypeStruct((t, d), BF16))
    return pl.pallas_call(
        functools.partial(_mlp_kernel, n_f=n_f, emit_u=emit_u),
        grid=(rows, n_f),
        in_specs=in_specs,
        out_specs=out_specs,
        out_shape=out_shape,
        scratch_shapes=[pltpu.VMEM((2, tm, d), F32)],
        compiler_params=_params(("arbitrary", "arbitrary")),
        name=name,
    )(*args)


def _retention_kernel(tab_ref, q_ref, k_ref, v_ref, g_ref, cos_ref, sin_ref, gnw_ref,
                      o_ref, state_ref, dmask_ref, xi_ref, zeta_ref, *, n_heads, blk):
    head = pl.program_id(2)
    half = RET_HEAD_DK // 2
    log_gamma = tab_ref[head]
    gamma_blk = tab_ref[n_heads + head]

    @pl.when(pl.program_id(1) == 0)
    def _():
        state_ref[head] = jnp.zeros(state_ref.shape[1:], F32)
        ri = lax.broadcasted_iota(jnp.int32, (blk, blk), 0)
        ci = lax.broadcasted_iota(jnp.int32, (blk, blk), 1)
        dist = jnp.abs(ri - ci).astype(F32)
        shift = int(math.log2(CHUNK))
        visible = jnp.right_shift(ci, shift) <= jnp.right_shift(ri, shift)
        dmask_ref[head] = jnp.where(visible, jnp.exp(dist * log_gamma), 0.0)
        pos = lax.broadcasted_iota(jnp.int32, (blk, V7X_LANES), 0).astype(F32)
        xi_ref[head] = jnp.exp((pos + 1.0) * log_gamma)
        zeta_ref[head] = jnp.exp((blk - 1.0 - pos) * log_gamma)

    xi = jnp.concatenate([xi_ref[head]] * (RET_HEAD_DV // V7X_LANES), axis=1)
    zeta = jnp.concatenate([zeta_ref[head]] * (RET_HEAD_DK // V7X_LANES), axis=1)
    dmask = dmask_ref[head]
    gnw = gnw_ref[...]

    for sub in range(q_ref.shape[0] // blk):
        rows = pl.ds(sub * blk, blk)
        cos = cos_ref[rows, :]
        sin = sin_ref[rows, :]

        def rotary(t):
            t1, t2 = t[:, :half], t[:, half:]
            return jnp.concatenate([t1 * cos - t2 * sin, t1 * sin + t2 * cos], axis=1)

        qr = rotary(q_ref[rows, :].astype(F32))
        kr = rotary(k_ref[rows, :].astype(F32)) * (RET_HEAD_DK ** -0.5)
        qb = qr.astype(BF16)
        kb = kr.astype(BF16)
        vb = v_ref[rows, :].astype(BF16)

        state = state_ref[head]
        state_b = state.astype(BF16)
        kz = (kr * zeta).astype(BF16)
        upd = lax.dot_general(kz, vb, (((0,), (0,)), ((), ())), preferred_element_type=F32)
        state_ref[head] = state * gamma_blk + upd

        hb = blk // 2
        for part in range(2):
            lo = part * hb
            qh = qb[lo:lo + hb, :]
            scores = lax.dot_general(qh, kb, (((1,), (1,)), ((), ())),
                                     preferred_element_type=F32)
            inner = jnp.dot((scores * dmask[lo:lo + hb, :]).astype(BF16), vb,
                            preferred_element_type=F32)
            cross = jnp.dot(qh, state_b, preferred_element_type=F32)
            o = inner + cross * xi[lo:lo + hb, :]
            mu = jnp.mean(o, axis=-1, keepdims=True)
            dev = o - mu
            var = jnp.mean(dev * dev, axis=-1, keepdims=True)
            normed = dev * lax.rsqrt(var + GN_EPS) * gnw
            out_rows = pl.ds(sub * blk + lo, hb)
            o_ref[out_rows, :] = (_silu(g_ref[out_rows, :].astype(F32)) * normed).astype(o_ref.dtype)


def _rotary_tables(seq, half, blk):
    inv_freq = ROPE_BASE ** (-jnp.arange(half, dtype=F32) / half)
    ang_blk = (jnp.arange(seq // blk) * blk).astype(F32)[:, None] * inv_freq[None, :]
    ang_in = jnp.arange(blk).astype(F32)[:, None] * inv_freq[None, :]
    cb, sb = jnp.cos(ang_blk)[:, None, :], jnp.sin(ang_blk)[:, None, :]
    ci, si = jnp.cos(ang_in)[None], jnp.sin(ang_in)[None]
    return (cb * ci - sb * si).reshape(seq, half), (sb * ci + cb * si).reshape(seq, half)


def retention_core(proj, gn_w, *, batch, seq, name):
    t = proj.shape[0]
    n_heads = proj.shape[1] // (2 * RET_HEAD_DK + 2 * RET_HEAD_DV)
    step, blk = RET_STEP, RET_BLK
    steps = seq // step
    dv_per_dk = RET_HEAD_DV // RET_HEAD_DK
    k_off = n_heads
    v_off = 2 * n_heads // dv_per_dk
    g_off = v_off + n_heads

    half = RET_HEAD_DK // 2
    cos, sin = _rotary_tables(seq, half, blk)
    log_gamma = jnp.log1p(-jnp.exp2(-5.0 - jnp.arange(n_heads, dtype=F32)))
    tab = jnp.concatenate([log_gamma, jnp.exp(blk * log_gamma)])

    row = lambda b, c, h: b * steps + c
    return pl.pallas_call(
        functools.partial(_retention_kernel, n_heads=n_heads, blk=blk),
        grid=(batch, steps, n_heads),
        in_specs=[pl.BlockSpec(memory_space=pltpu.SMEM),
                  pl.BlockSpec((step, RET_HEAD_DK), lambda b, c, h: (row(b, c, h), h)),
                  pl.BlockSpec((step, RET_HEAD_DK), lambda b, c, h: (row(b, c, h), k_off + h)),
                  pl.BlockSpec((step, RET_HEAD_DV), lambda b, c, h: (row(b, c, h), v_off + h)),
                  pl.BlockSpec((step, RET_HEAD_DV), lambda b, c, h: (row(b, c, h), g_off + h)),
                  pl.BlockSpec((step, half), lambda b, c, h: (c, 0)),
                  pl.BlockSpec((step, half), lambda b, c, h: (c, 0)),
                  pl.BlockSpec((1, RET_HEAD_DV), lambda b, c, h: (0, h))],
        out_specs=pl.BlockSpec((step, RET_HEAD_DV), lambda b, c, h: (row(b, c, h), h)),
        out_shape=jax.ShapeDtypeStruct((t, n_heads * RET_HEAD_DV), BF16),
        scratch_shapes=[pltpu.VMEM((n_heads, RET_HEAD_DK, RET_HEAD_DV), F32),
                        pltpu.VMEM((n_heads, blk, blk), F32),
                        pltpu.VMEM((n_heads, blk, V7X_LANES), F32),
                        pltpu.VMEM((n_heads, blk, V7X_LANES), F32)],
        compiler_params=_params(("parallel", "arbitrary", "arbitrary")),
        name=name,
    )(tab, proj, proj, proj, proj, cos, sin, gn_w.reshape(1, -1))


def _softplus(x):
    return jnp.maximum(x, 0.0) + jnp.log1p(jnp.exp(-jnp.abs(x)))


def _split_bf16(v, terms):
    parts = []
    for _ in range(terms - 1):
        part = v.astype(BF16)
        parts.append(part)
        v = v - part.astype(F32)
    parts.append(v.astype(BF16))
    return jnp.concatenate(parts, axis=0)


def _sum_terms(stacked, rows):
    out = stacked[0:rows]
    for i in range(1, stacked.shape[0] // rows):
        out = out + stacked[i * rows:(i + 1) * rows]
    return out


def _ssd_decay_kernel(dt_ref, prm_ref, src_ref, acum_ref, inw_ref, *, blk):
    prm = prm_ref[...]
    a2 = -jnp.exp(prm[:, 1:2]) * LOG2_E
    ri = lax.broadcasted_iota(jnp.int32, (blk, blk), 0)
    ci = lax.broadcasted_iota(jnp.int32, (blk, blk), 1)
    upper_ones = jnp.where(ri <= ci, 1.0, 0.0).astype(BF16)
    n_heads = dt_ref.shape[0]
    for sub in range(dt_ref.shape[1] // blk):
        cols = pl.ds(sub * blk, blk)
        dt = _softplus(dt_ref[:, cols] + prm[:, 0:1])
        acum = _sum_terms(jnp.dot(_split_bf16(dt * a2, 3), upper_ones,
                                  preferred_element_type=F32), n_heads)
        src_ref[:, cols] = acum - jnp.log2(dt)
        acum_ref[:, cols] = acum
        inw_ref[:, cols] = dt * jnp.exp2(acum[:, blk - 1:blk] - acum)


def ssd_decays(dt_raw_t, dt_bias, a_log, *, blk, name):
    n_heads, t = dt_raw_t.shape
    step = SSD_DECAY_STEP
    spec = pl.BlockSpec((n_heads, step), lambda i: (0, i))
    shape = jax.ShapeDtypeStruct((n_heads, t), F32)
    return pl.pallas_call(
        functools.partial(_ssd_decay_kernel, blk=blk),
        grid=(t // step,),
        in_specs=[spec, pl.BlockSpec((n_heads, 2), lambda i: (0, 0))],
        out_specs=[spec, spec, spec],
        out_shape=[shape, shape, shape],
        compiler_params=_params(("parallel",)),
        name=name,
    )(dt_raw_t, jnp.stack([dt_bias, a_log], axis=-1))


def _ssd_kernel(z_ref, x_ref, b_ref, c_ref, wx_ref, wb_ref, wc_ref, bx_ref, bb_ref, bc_ref,
                src_ref, acum_ref, inw_ref, dexp_ref, nw_ref, o_ref,
                state_ref, xe_ref, be_ref, ce_ref, *, blk):
    step_rows = x_ref.shape[0]
    halo = V7X_SUBLANES

    @pl.when(pl.program_id(2) == 0)
    def _():
        state_ref[...] = jnp.zeros_like(state_ref)
        xe_ref[0:halo, :] = jnp.zeros((halo, xe_ref.shape[1]), F32)
        be_ref[0:halo, :] = jnp.zeros((halo, be_ref.shape[1]), F32)
        ce_ref[0:halo, :] = jnp.zeros((halo, ce_ref.shape[1]), F32)

    xe_ref[halo:, :] = x_ref[...].astype(F32)
    be_ref[halo:, :] = b_ref[...].astype(F32)
    ce_ref[halo:, :] = c_ref[...].astype(F32)

    def conv_silu(ext_ref, w_ref, bias_ref, off):
        out = bias_ref[...] + ext_ref[pl.ds(halo + off, blk), :] * w_ref[SSD_CONV_W - 1:SSD_CONV_W, :]
        for s in range(1, SSD_CONV_W):
            out = out + (ext_ref[pl.ds(halo + off - s, blk), :]
                         * w_ref[SSD_CONV_W - 1 - s:SSD_CONV_W - s, :])
        return _silu(out)

    ri = lax.broadcasted_iota(jnp.int32, (blk, blk), 0)
    ci = lax.broadcasted_iota(jnp.int32, (blk, blk), 1)
    causal = ri >= ci
    lane = lax.broadcasted_iota(jnp.int32, (blk, V7X_LANES), 1)
    first = lane < SSD_HEADDIM
    keep_first = jnp.where(first, 1.0, 0.0).astype(BF16)
    keep_second = jnp.where(first, 0.0, 1.0).astype(BF16)
    dexp = dexp_ref[...]
    nw = nw_ref[...]
    n_heads = acum_ref.shape[0]
    gw = x_ref.shape[1]

    def head_selector(terms, lanes_per_head):
        k = lax.broadcasted_iota(jnp.int32, (terms * n_heads, n_heads * lanes_per_head), 0)
        n = lax.broadcasted_iota(jnp.int32, (terms * n_heads, n_heads * lanes_per_head), 1)
        shift = int(math.log2(lanes_per_head))
        return jnp.where((k & (n_heads - 1)) == jnp.right_shift(n, shift), 1.0, 0.0).astype(BF16)

    sel_mask = head_selector(3, blk)
    sel_chan = head_selector(2, SSD_HEADDIM)
    transposed_lhs = (((0,), (0,)), ((), ()))

    for sub in range(step_rows // blk):
        off = sub * blk
        src_r = src_ref[:, pl.ds(off, blk)]
        acum_r = acum_ref[:, pl.ds(off, blk)]
        acum_terms = _split_bf16(acum_r, 3)
        in_weight = lax.dot_general(_split_bf16(inw_ref[:, pl.ds(off, blk)], 2), sel_chan,
                                    transposed_lhs, preferred_element_type=F32)
        xs = conv_silu(xe_ref, wx_ref, bx_ref, off)
        bmb = conv_silu(be_ref, wb_ref, bb_ref, off).astype(BF16)
        cmb = conv_silu(ce_ref, wc_ref, bc_ref, off).astype(BF16)

        cb = lax.dot_general(cmb, bmb, (((1,), (1,)), ((), ())), preferred_element_type=F32)
        state = state_ref[...]
        y_off = jnp.dot(cmb, state.astype(BF16), preferred_element_type=F32)

        y_tiles, decay_tiles = [], []
        for pair in range(gw // V7X_LANES):
            r0, r1 = 2 * pair, 2 * pair + 1
            cols = slice(pair * V7X_LANES, (pair + 1) * V7X_LANES)
            xp = xs[:, cols]
            xpb = xp.astype(BF16)
            acum_bc = lax.dot_general(acum_terms, sel_mask[:, r0 * blk:(r1 + 1) * blk],
                                      transposed_lhs, preferred_element_type=F32)
            out_decay = jnp.exp2(jnp.where(first, acum_bc[:, 0:V7X_LANES],
                                           acum_bc[:, blk:blk + V7X_LANES]))
            y = y_off[:, cols] * out_decay + dexp[:, cols] * xp
            for k, (r, keep) in enumerate(((r0, keep_first), (r1, keep_second))):
                seg = acum_bc[:, k * blk:(k + 1) * blk] - src_r[r:r + 1, :]
                lmat = jnp.exp2(jnp.where(causal, seg, -jnp.inf))
                y = y + jnp.dot((cb * lmat).astype(BF16), xpb * keep,
                                preferred_element_type=F32)
            y_tiles.append(y)
            decay_tiles.append(out_decay[blk - 1:blk, :])
        y = jnp.concatenate(y_tiles, axis=1)
        xw = (xs * in_weight).astype(BF16)
        upd = lax.dot_general(bmb, xw, transposed_lhs, preferred_element_type=F32)
        state_ref[...] = state * jnp.concatenate(decay_tiles, axis=1) + upd

        yg = y * _silu(z_ref[pl.ds(off, blk), :].astype(F32))
        o_ref[pl.ds(off, blk), :] = (yg * _rms_scale(yg) * nw).astype(o_ref.dtype)

    xe_ref[0:halo, :] = xe_ref[step_rows:step_rows + halo, :]
    be_ref[0:halo, :] = be_ref[step_rows:step_rows + halo, :]
    ce_ref[0:halo, :] = ce_ref[step_rows:step_rows + halo, :]


def ssd_core(zxbc, dt_raw_t, conv_w, conv_b, dt_bias, a_log, d_skip, norm_w, *, batch, seq, name):
    t = zxbc.shape[0]
    g, r, n, gw = SSD_GROUPS, SSD_HEADS_PER_GROUP, SSD_STATE, SSD_GROUP_WIDTH
    d_inner = g * gw
    step, blk = SSD_STEP, SSD_BLK
    steps = seq // step
    x_off = d_inner // gw
    b_off = 2 * d_inner // n
    c_off = b_off + g
    wb_off = d_inner // n
    wc_off = wb_off + g

    src_r, acum_r, inw_r = (a.reshape(g, r, t) for a in
                           ssd_decays(dt_raw_t, dt_bias, a_log, blk=blk, name=name + "_decays"))
    dexp = jnp.repeat(d_skip, SSD_HEADDIM).reshape(1, d_inner)
    conv_b = conv_b.reshape(1, -1)

    row = lambda b, gi, c: b * steps + c
    return pl.pallas_call(
        functools.partial(_ssd_kernel, blk=blk),
        grid=(batch, g, steps),
        in_specs=[pl.BlockSpec((step, gw), lambda b, gi, c: (row(b, gi, c), gi)),
                  pl.BlockSpec((step, gw), lambda b, gi, c: (row(b, gi, c), x_off + gi)),
                  pl.BlockSpec((step, n), lambda b, gi, c: (row(b, gi, c), b_off + gi)),
                  pl.BlockSpec((step, n), lambda b, gi, c: (row(b, gi, c), c_off + gi)),
                  pl.BlockSpec((SSD_CONV_W, gw), lambda b, gi, c: (0, gi)),
                  pl.BlockSpec((SSD_CONV_W, n), lambda b, gi, c: (0, wb_off + gi)),
                  pl.BlockSpec((SSD_CONV_W, n), lambda b, gi, c: (0, wc_off + gi)),
                  pl.BlockSpec((1, gw), lambda b, gi, c: (0, gi)),
                  pl.BlockSpec((1, n), lambda b, gi, c: (0, wb_off + gi)),
                  pl.BlockSpec((1, n), lambda b, gi, c: (0, wc_off + gi)),
                  pl.BlockSpec((None, r, step), lambda b, gi, c: (gi, 0, row(b, gi, c))),
                  pl.BlockSpec((None, r, step), lambda b, gi, c: (gi, 0, row(b, gi, c))),
                  pl.BlockSpec((None, r, step), lambda b, gi, c: (gi, 0, row(b, gi, c))),
                  pl.BlockSpec((1, gw), lambda b, gi, c: (0, gi)),
                  pl.BlockSpec((1, gw), lambda b, gi, c: (0, gi))],
        out_specs=pl.BlockSpec((step, gw), lambda b, gi, c: (row(b, gi, c), gi)),
        out_shape=jax.ShapeDtypeStruct((t, d_inner), BF16),
        scratch_shapes=[pltpu.VMEM((n, gw), F32),
                        pltpu.VMEM((step + V7X_SUBLANES, gw), F32),
                        pltpu.VMEM((step + V7X_SUBLANES, n), F32),
                        pltpu.VMEM((step + V7X_SUBLANES, n), F32)],
        compiler_params=_params(("parallel", "parallel", "arbitrary")),
        name=name,
    )(zxbc, zxbc, zxbc, zxbc, conv_w, conv_w, conv_w, conv_b, conv_b, conv_b,
      src_r, acum_r, inw_r, dexp, norm_w.reshape(1, d_inner))


def kernel(x, norm_mix_pre, norm_mix_post, norm_ffn_pre, norm_ffn_post, ret_w_in, ret_gn_w, ret_w_out, ssd_w_in, ssd_conv_w, ssd_conv_b, ssd_dt_bias, ssd_a_log, ssd_d, ssd_norm_w, ssd_w_out, mlp_w_up, mlp_w_down):
    batch, seq, d = x.shape
    h = x.reshape(batch * seq, d)

    n_zxbc = ssd_w_in.shape[2] - SSD_GROUPS * SSD_HEADS_PER_GROUP
    ssd_w_in_t = jnp.swapaxes(ssd_w_in, 1, 2)
    ret_w_in, ret_w_out, ssd_w_out = (w.astype(BF16) for w in (ret_w_in, ret_w_out, ssd_w_out))

    proj = norm_matmul(h, norm_mix_pre, 0, ret_w_in, 0, out_dtype=BF16, name="ret_in_proj")
    y = retention_core(proj, ret_gn_w[0], batch=batch, seq=seq, name="retention_core")
    h, u = out_proj_residual(y, ret_w_out, 0, h, norm_mix_post, 0, norm_ffn_pre, 0,
                             name="ret_out_proj")
    h, u = mlp_residual(u, h, mlp_w_up, mlp_w_down, norm_ffn_post, 0, norm_mix_pre, 1, name="mlp0")

    zxbc, dt_raw_t = matmul_dt(u, ssd_w_in_t, 0, n_out=n_zxbc, out_dtype=F32,
                               name="ssd_in_proj")
    y = ssd_core(zxbc, dt_raw_t, ssd_conv_w[0], ssd_conv_b[0], ssd_dt_bias[0], ssd_a_log[0],
                 ssd_d[0], ssd_norm_w[0], batch=batch, seq=seq, name="ssd_core")
    h, u = out_proj_residual(y, ssd_w_out, 0, h, norm_mix_post, 1, norm_ffn_pre, 1,
                             name="ssd_out_proj")
    (h,) = mlp_residual(u, h, mlp_w_up, mlp_w_down, norm_ffn_post, 1, name="mlp1")
    return h.reshape(batch, seq, d)
```
